```python
import jax, jax.numpy as jnp
from jax import lax
import numpy as np

D_MODEL = 1024
BATCH = 8
SEQ = 4096
DEPTH = 2
DEC_BATCH = 32
DEC_SEQ = 1
PAST_LEN = 16384
PAGE_SIZE = 128

N_MIXERS = 2
N_HEADS = 16
HEAD_DIM = 64
N_KV_HEADS = 4
GQA_GROUP = N_HEADS // N_KV_HEADS
ROT_DIM = HEAD_DIM // 4
ROPE_THETA = 500000.0
IDX_HEADS = 8
IDX_DIM = 64
IDX_ROT = IDX_DIM // 4
TOPK_MAX = 256
Q_BLOCK = 128
POOL_WINDOWS = (2, 4, 8, 16)
POOL_GROUPS = len(POOL_WINDOWS)
POOL_CH = D_MODEL // POOL_GROUPS
POOL_BUF = max(POOL_WINDOWS) - 1
D_FF = 2816
LN_EPS = 1e-5
ALPHA = (2 * DEPTH) ** 0.25
BETA = (8 * DEPTH) ** -0.25
N_ATTN_LAYERS = (DEPTH + N_MIXERS - 1) // N_MIXERS
N_POOL_LAYERS = DEPTH // N_MIXERS
Q_COLS = N_HEADS * HEAD_DIM
KV_COLS = N_KV_HEADS * HEAD_DIM
QI_COLS = IDX_HEADS * IDX_DIM
D_IN_ATTN = Q_COLS + 2 * KV_COLS + QI_COLS + IDX_DIM + IDX_HEADS

kernel_name = "dsa_pool_macaron_deepnorm_step"


def _layernorm(x, g, b):
    xf = x.astype(jnp.float32)
    mu = jnp.mean(xf, axis=-1, keepdims=True)
    var = jnp.mean(jnp.square(xf - mu), axis=-1, keepdims=True)
    y = (xf - mu) * lax.rsqrt(var + LN_EPS) * g.astype(jnp.float32) + b.astype(jnp.float32)
    return y.astype(x.dtype)


def _swiglu(x, w_i, w_o):
    h = jnp.einsum('btd,df->btf', x, w_i)
    gate, up = h[..., :D_FF], h[..., D_FF:]
    return jnp.einsum('btf,fd->btd', jax.nn.silu(gate) * up, w_o)


def _rope(x, pos, rot):
    half = rot // 2
    freqs = ROPE_THETA ** (-jnp.arange(half, dtype=jnp.float32) / half)
    ang = pos[:, None] * freqs[None, :]
    cos = jnp.cos(ang)[None, :, None, :]
    sin = jnp.sin(ang)[None, :, None, :]
    xr = x[..., :rot].astype(jnp.float32)
    x1, x2 = xr[..., :half], xr[..., half:]
    r = jnp.concatenate([x1 * cos - x2 * sin, x2 * cos + x1 * sin], axis=-1).astype(x.dtype)
    return jnp.concatenate([r, x[..., rot:]], axis=-1)


def _attn_project(x, w_in, pos):
    B, T, _ = x.shape
    p = jnp.einsum('btd,de->bte', x, w_in)
    o1 = Q_COLS
    o2 = o1 + KV_COLS
    o3 = o2 + KV_COLS
    o4 = o3 + QI_COLS
    o5 = o4 + IDX_DIM
    q = _rope(p[..., :o1].reshape(B, T, N_HEADS, HEAD_DIM), pos, ROT_DIM)
    k = _rope(p[..., o1:o2].reshape(B, T, N_KV_HEADS, HEAD_DIM), pos, ROT_DIM)
    v = p[..., o2:o3].reshape(B, T, N_KV_HEADS, HEAD_DIM)
    qi = _rope(p[..., o3:o4].reshape(B, T, IDX_HEADS, IDX_DIM), pos, IDX_ROT)
    ki = _rope(p[..., o4:o5][:, :, None, :], pos, IDX_ROT)[:, :, 0, :]
    wi = p[..., o5:]
    return q, k, v, qi, ki, wi


def _indexer_select(qi, wi, ki, qpos, topk):
    s = jnp.einsum('bqhd,bsd->bqhs', qi.astype(jnp.float32), ki.astype(jnp.float32)) * (IDX_DIM ** -0.5)
    score = jnp.einsum('bqhs,bqh->bqs', jax.nn.relu(s), wi.astype(jnp.float32) * (IDX_HEADS ** -0.5))
    kpos = jnp.arange(ki.shape[1], dtype=jnp.int32)
    causal = kpos[None, :] <= qpos[:, None]
    score = jnp.where(causal[None], score, -jnp.inf)
    _, idx = lax.top_k(score, topk)
    valid = idx <= qpos[None, :, None]
    return idx, valid


def _sparse_attend(q, ks, vs, valid):
    B, Q, _, _ = q.shape
    qg = q.reshape(B, Q, N_KV_HEADS, GQA_GROUP, HEAD_DIM)
    s = jnp.einsum('bqhgd,bqnhd->bqhgn', qg, ks).astype(jnp.float32) * (HEAD_DIM ** -0.5)
    s = jnp.where(valid[:, :, None, None, :], s, -jnp.inf)
    p = jax.nn.softmax(s, axis=-1).astype(vs.dtype)
    o = jnp.einsum('bqhgn,bqnhd->bqhgd', p, vs)
    return o.reshape(B, Q, N_HEADS * HEAD_DIM)


def _dsa_prompt(x, w_in, w_o):
    B, S, _ = x.shape
    pos_i = jnp.arange(S, dtype=jnp.int32)
    q, k, v, qi, ki, wi = _attn_project(x, w_in, pos_i.astype(jnp.float32))
    topk = min(TOPK_MAX, S // 4)
    nb = S // Q_BLOCK
    bidx = jnp.arange(B)[:, None, None]

    def to_blocks(a):
        return jnp.moveaxis(a.reshape((B, nb, Q_BLOCK) + a.shape[2:]), 1, 0)

    def block(args):
        qb, qib, wib, pb = args
        idx, valid = _indexer_select(qib, wib, ki, pb, topk)
        return _sparse_attend(qb, k[bidx, idx], v[bidx, idx], valid)

    out = lax.map(block, (to_blocks(q), to_blocks(qi), to_blocks(wi), pos_i.reshape(nb, Q_BLOCK)))
    out = jnp.moveaxis(out, 0, 1).reshape(B, S, N_HEADS * HEAD_DIM)
    return jnp.einsum('bte,ed->btd', out, w_o), k, v, ki


def _dsa_sample(x, ck, cv, cki, page_table, w_in, w_o):
    B, T, _ = x.shape
    n_pages = page_table.shape[1]
    P = n_pages * PAGE_SIZE
    pos_i = P + jnp.arange(T, dtype=jnp.int32)
    q, k, v, qi, ki, wi = _attn_project(x, w_in, pos_i.astype(jnp.float32))
    ki_past = cki[page_table].reshape(B, P, IDX_DIM)
    ki_all = jnp.concatenate([ki_past, ki], axis=1)
    topk = min(TOPK_MAX, (P + T) // 4)
    idx, valid = _indexer_select(qi, wi, ki_all, pos_i, topk)
    in_past = idx < P
    pidx = jnp.minimum(idx, P - 1)
    phys = jnp.take_along_axis(page_table, (pidx // PAGE_SIZE).reshape(B, -1), axis=1).reshape(idx.shape)
    off = pidx % PAGE_SIZE
    bidx = jnp.arange(B)[:, None, None]
    nidx = jnp.clip(idx - P, 0, T - 1)
    sel = in_past[..., None, None]
    ks = jnp.where(sel, ck[phys, off], k[bidx, nidx])
    vs = jnp.where(sel, cv[phys, off], v[bidx, nidx])
    out = _sparse_attend(q, ks, vs, valid)
    return jnp.einsum('bte,ed->btd', out, w_o), k, v, ki


def _multiscale_pool(x_ext, n_prefix, w_pool, scale):
    B, L, D = x_ext.shape
    xf = x_ext.astype(jnp.float32)
    cs = jnp.concatenate([jnp.zeros((B, 1, D), jnp.float32), jnp.cumsum(xf, axis=1)], axis=1)
    i = jnp.arange(n_prefix, L)
    diffs = []
    for g, w in enumerate(POOL_WINDOWS):
        sl = slice(g * POOL_CH, (g + 1) * POOL_CH)
        cs_g = cs[..., sl]
        lo = jnp.maximum(i + 1 - w, 0)
        cnt = (i + 1 - lo).astype(jnp.float32)
        mean = (cs_g[:, i + 1] - cs_g[:, lo]) / cnt[None, :, None]
        diffs.append(mean - xf[:, n_prefix:, sl])
    d = jnp.stack(diffs, axis=2).astype(x_ext.dtype)
    out = jnp.einsum('btgc,gce->btge', d, w_pool).reshape(B, L - n_prefix, D)
    return out * scale


def setup_inputs(seed: int = 0) -> dict:
    key = jax.random.key(seed)
    ks = jax.random.split(key, 20)
    f32 = jnp.float32
    n_pages = PAST_LEN // PAGE_SIZE
    n_used = DEC_BATCH * n_pages
    n_phys = n_used + n_used // 4
    page_table = jax.random.permutation(ks[0], n_phys)[:n_used].reshape(DEC_BATCH, n_pages).astype(jnp.int32)
    return {
        "x_prompt": jax.random.normal(ks[1], (BATCH, SEQ, D_MODEL), f32),
        "x_sample": jax.random.normal(ks[2], (DEC_BATCH, DEC_SEQ, D_MODEL), f32),
        "cache_k": jax.random.normal(ks[3], (N_ATTN_LAYERS, n_phys, PAGE_SIZE, N_KV_HEADS, HEAD_DIM), f32),
        "cache_v": jax.random.normal(ks[4], (N_ATTN_LAYERS, n_phys, PAGE_SIZE, N_KV_HEADS, HEAD_DIM), f32),
        "cache_kidx": jax.random.normal(ks[5], (N_ATTN_LAYERS, n_phys, PAGE_SIZE, IDX_DIM), f32),
        "state_pool": jax.random.normal(ks[6], (N_POOL_LAYERS, DEC_BATCH, POOL_BUF, D_MODEL), f32),
        "page_table": page_table,
        "ln_g": 1.0 + 0.02 * jax.random.normal(ks[7], (DEPTH, 3, D_MODEL), f32),
        "ln_b": 0.02 * jax.random.normal(ks[8], (DEPTH, 3, D_MODEL), f32),
        "ffn1_wi": jax.random.normal(ks[9], (DEPTH, D_MODEL, 2 * D_FF), f32) * D_MODEL ** -0.5,
        "ffn1_wo": jax.random.normal(ks[10], (DEPTH, D_FF, D_MODEL), f32) * (D_FF ** -0.5 * BETA),
        "ffn2_wi": jax.random.normal(ks[11], (DEPTH, D_MODEL, 2 * D_FF), f32) * D_MODEL ** -0.5,
        "ffn2_wo": jax.random.normal(ks[12], (DEPTH, D_FF, D_MODEL), f32) * (D_FF ** -0.5 * BETA),
        "attn_w_in": jax.random.normal(ks[13], (N_ATTN_LAYERS, D_MODEL, D_IN_ATTN), f32) * D_MODEL ** -0.5,
        "attn_w_o": jax.random.normal(ks[14], (N_ATTN_LAYERS, Q_COLS, D_MODEL), f32) * (Q_COLS ** -0.5 * BETA),
        "pool_w": jax.random.normal(ks[15], (N_POOL_LAYERS, POOL_GROUPS, POOL_CH, POOL_CH), f32) * (POOL_CH ** -0.5 * BETA),
        "pool_scale": 1.0 + 0.02 * jax.random.normal(ks[16], (N_POOL_LAYERS, D_MODEL), f32),
    }


def reference(x_prompt, x_sample, cache_k, cache_v, cache_kidx, state_pool, page_table,
              ln_g, ln_b, ffn1_wi, ffn1_wo, ffn2_wi, ffn2_wo,
              attn_w_in, attn_w_o, pool_w, pool_scale):
    yp, ys = x_prompt, x_sample
    kp_l, vp_l, kip_l, poolp_l = [], [], [], []
    ks_l, vs_l, kis_l, pools_l = [], [], [], []
    for l in range(DEPTH):
        j = l // N_MIXERS
        yp = _layernorm(ALPHA * yp + 0.5 * _swiglu(yp, ffn1_wi[l], ffn1_wo[l]), ln_g[l, 0], ln_b[l, 0])
        ys = _layernorm(ALPHA * ys + 0.5 * _swiglu(ys, ffn1_wi[l], ffn1_wo[l]), ln_g[l, 0], ln_b[l, 0])
        if l % N_MIXERS == 0:
            mp, kp, vp, kip = _dsa_prompt(yp, attn_w_in[j], attn_w_o[j])
            ms, kn, vn, kin = _dsa_sample(ys, cache_k[j], cache_v[j], cache_kidx[j], page_table,
                                          attn_w_in[j], attn_w_o[j])
            kp_l.append(kp); vp_l.append(vp); kip_l.append(kip)
            ks_l.append(kn); vs_l.append(vn); kis_l.append(kin)
        else:
            mp = _multiscale_pool(yp, 0, pool_w[j], pool_scale[j])
            poolp_l.append(yp[:, yp.shape[1] - POOL_BUF:])
            ext = jnp.concatenate([state_pool[j], ys], axis=1)
            ms = _multiscale_pool(ext, POOL_BUF, pool_w[j], pool_scale[j])
            pools_l.append(ext[:, ext.shape[1] - POOL_BUF:])
        yp = _layernorm(ALPHA * yp + mp, ln_g[l, 1], ln_b[l, 1])
        ys = _layernorm(ALPHA * ys + ms, ln_g[l, 1], ln_b[l, 1])
        yp = _layernorm(ALPHA * yp + 0.5 * _swiglu(yp, ffn2_wi[l], ffn2_wo[l]), ln_g[l, 2], ln_b[l, 2])
        ys = _layernorm(ALPHA * ys + 0.5 * _swiglu(ys, ffn2_wi[l], ffn2_wo[l]), ln_g[l, 2], ln_b[l, 2])
    return (yp, ys,
            jnp.stack(kp_l), jnp.stack(vp_l), jnp.stack(kip_l), jnp.stack(poolp_l),
            jnp.stack(ks_l), jnp.stack(vs_l), jnp.stack(kis_l), jnp.stack(pools_l))
```

```python
import functools

import jax
import jax.numpy as jnp
from jax import lax
from jax.experimental import pallas as pl
from jax.experimental.pallas import tpu as pltpu

N_MIXERS = 2
N_HEADS = 16
HEAD_DIM = 64
N_KV_HEADS = 4
GQA_GROUP = N_HEADS // N_KV_HEADS
ROT_DIM = HEAD_DIM // 4
ROPE_THETA = 500000.0
IDX_HEADS = 8
IDX_DIM = 64
TOPK_MAX = 256
PAGE_SIZE = 128
POOL_WINDOWS = (2, 4, 8, 16)
POOL_BUF = max(POOL_WINDOWS) - 1
LN_EPS = 1e-5
Q_COLS = N_HEADS * HEAD_DIM
KV_COLS = N_KV_HEADS * HEAD_DIM
QI_COLS = IDX_HEADS * IDX_DIM

LANES = 128
SUBLANES = 8
VMEM_LIMIT_BYTES = 56 * 1024 * 1024

F32 = jnp.float32
BF16 = jnp.bfloat16
INT_MIN = -(2 ** 31)
NEG_BIG = -1e30

_NT = (((1,), (1,)), ((), ()))


def _layernorm(y, g, b):
    mu = jnp.mean(y, axis=-1, keepdims=True)
    yc = y - mu
    var = jnp.mean(yc * yc, axis=-1, keepdims=True)
    return yc * lax.rsqrt(var + LN_EPS) * g + b


def _const_spec(shape):
    nd = len(shape)
    return pl.BlockSpec(shape, lambda *_: (0,) * nd, pipeline_mode=pl.Buffered(1))


def _params(n_grid):
    return pltpu.CompilerParams(dimension_semantics=("arbitrary",) * n_grid,
                                vmem_limit_bytes=VMEM_LIMIT_BYTES)


FFN_CHUNK = 256


def _ffn_kernel(x_ref, wi_ref, wo_ref, g_ref, b_ref, o_ref, acc_ref, *, d_ff, alpha):
    x = x_ref[...]
    xb = x.astype(BF16)
    for c in range(d_ff // FFN_CHUNK):
        lo = c * FFN_CHUNK
        hg = jnp.dot(xb, wi_ref[:, lo:lo + FFN_CHUNK], preferred_element_type=F32)
        hu = jnp.dot(xb, wi_ref[:, d_ff + lo:d_ff + lo + FFN_CHUNK], preferred_element_type=F32)
        a = (hg * jax.nn.sigmoid(hg) * hu).astype(BF16)
        part = jnp.dot(a, wo_ref[lo:lo + FFN_CHUNK, :], preferred_element_type=F32)
        if c == 0:
            acc_ref[...] = part
        else:
            acc_ref[...] += part
    o_ref[...] = _layernorm(alpha * x + 0.5 * acc_ref[...], g_ref[...], b_ref[...])


def _ffn_ln(x, wi, wo, g, b, *, tm, alpha):
    m, d = x.shape
    d_ff = wo.shape[0]
    assert m % tm == 0 and d_ff % FFN_CHUNK == 0
    return pl.pallas_call(
        functools.partial(_ffn_kernel, d_ff=d_ff, alpha=alpha),
        grid=(m // tm,),
        in_specs=[pl.BlockSpec((tm, d), lambda i: (i, 0)),
                  _const_spec(wi.shape), _const_spec(wo.shape),
                  _const_spec((1, d)), _const_spec((1, d))],
        out_specs=pl.BlockSpec((tm, d), lambda i: (i, 0)),
        out_shape=jax.ShapeDtypeStruct((m, d), F32),
        scratch_shapes=[pltpu.VMEM((tm, d), F32)],
        compiler_params=_params(1),
    )(x, wi, wo, g.reshape(1, d), b.reshape(1, d))


O_K = Q_COLS
O_V = O_K + KV_COLS
O_QI = O_V + KV_COLS
O_KI = O_QI + QI_COLS
W_IN_PAD = O_KI + LANES


def _proj_kernel(x_ref, w_ref, cos_ref, sa_ref, sb_ref,
                 q_ref, k_ref, v_ref, kab_ref, vab_ref, qi_ref, kiab_ref, kiw_ref):
    xb = x_ref[...].astype(BF16)
    cos = cos_ref[...]
    sin_a = sa_ref[...]
    sin_b = sb_ref[...]
    tm = xb.shape[0]
    lane = lax.broadcasted_iota(jnp.int32, (tm, LANES), 1)
    low = lane < HEAD_DIM

    def rope(t):
        return (t * cos + pltpu.roll(t, ROT_DIM // 2, 1) * sin_a
                + pltpu.roll(t, LANES - ROT_DIM // 2, 1) * sin_b)

    def proj(lo, width):
        return jnp.dot(xb, w_ref[:, lo:lo + width], preferred_element_type=F32)

    pq = proj(0, Q_COLS)
    for t in range(Q_COLS // LANES):
        q_ref[:, t * LANES:(t + 1) * LANES] = (
            rope(pq[:, t * LANES:(t + 1) * LANES]) * (HEAD_DIM ** -0.5)).astype(BF16)

    pk = proj(O_K, KV_COLS)
    pv = proj(O_V, KV_COLS)
    ones_a = jnp.where(lane == HEAD_DIM, 1.0, 0.0)
    ones_b = jnp.where(lane == 0, 1.0, 0.0)
    for t in range(KV_COLS // LANES):
        kr = rope(pk[:, t * LANES:(t + 1) * LANES])
        k_ref[:, t * LANES:(t + 1) * LANES] = kr
        krr = pltpu.roll(kr, HEAD_DIM, 1)
        g0, g1 = 2 * t, 2 * t + 1
        kab_ref[2 * g0] = jnp.where(low, kr, 0.0).astype(BF16)
        kab_ref[2 * g0 + 1] = jnp.where(low, 0.0, krr).astype(BF16)
        kab_ref[2 * g1] = jnp.where(low, krr, 0.0).astype(BF16)
        kab_ref[2 * g1 + 1] = jnp.where(low, 0.0, kr).astype(BF16)
        vv = pv[:, t * LANES:(t + 1) * LANES]
        v_ref[:, t * LANES:(t + 1) * LANES] = vv
        vvr = pltpu.roll(vv, HEAD_DIM, 1)
        vab_ref[2 * g0] = jnp.where(low, vv, ones_a).astype(BF16)
        vab_ref[2 * g0 + 1] = jnp.where(low, ones_b, vvr).astype(BF16)
        vab_ref[2 * g1] = jnp.where(low, vvr, ones_a).astype(BF16)
        vab_ref[2 * g1 + 1] = jnp.where(low, ones_b, vv).astype(BF16)

    pqi = proj(O_QI, QI_COLS)
    for t in range(QI_COLS // LANES):
        qi_ref[:, t * LANES:(t + 1) * LANES] = rope(pqi[:, t * LANES:(t + 1) * LANES]).astype(BF16)

    pkw = proj(O_KI, LANES)
    kw = jnp.where(low, rope(pkw), pkw)
    kiw_ref[...] = kw
    kiab_ref[0] = jnp.where(low, kw, 0.0).astype(BF16)
    kiab_ref[1] = jnp.where(low, 0.0, pltpu.roll(kw, HEAD_DIM, 1)).astype(BF16)


def _rope_tables(pos):
    half = ROT_DIM // 2
    freqs = ROPE_THETA ** (-jnp.arange(half, dtype=F32) / half)
    ang = pos[:, None] * freqs[None, :]
    cos, sin = jnp.cos(ang), jnp.sin(ang)
    t = pos.shape[0]
    rest = HEAD_DIM - ROT_DIM
    zeros_h = jnp.zeros((t, half), F32)
    c64 = jnp.concatenate([cos, cos, jnp.ones((t, rest), F32)], axis=1)
    a64 = jnp.concatenate([zeros_h, sin, jnp.zeros((t, rest), F32)], axis=1)
    b64 = jnp.concatenate([-sin, zeros_h, jnp.zeros((t, rest), F32)], axis=1)
    rep = LANES // HEAD_DIM
    return tuple(jnp.tile(a, (1, rep)) for a in (c64, a64, b64))


def _attn_proj(x, w_pad, tables, *, tm):
    m, d = x.shape
    t_rows = tables[0].shape[0]
    assert m % tm == 0 and t_rows % tm == 0
    n_t = t_rows // tm
    row = lambda i: (i, 0)
    row3 = lambda i: (0, i, 0)
    tab_spec = pl.BlockSpec((tm, LANES), lambda i: (i % n_t, 0))
    n_kv2 = 2 * N_KV_HEADS
    return pl.pallas_call(
        _proj_kernel,
        grid=(m // tm,),
        in_specs=[pl.BlockSpec((tm, d), row), _const_spec(w_pad.shape), tab_spec, tab_spec, tab_spec],
        out_specs=[pl.BlockSpec((tm, Q_COLS), row),
                   pl.BlockSpec((tm, KV_COLS), row),
                   pl.BlockSpec((tm, KV_COLS), row),
                   pl.BlockSpec((n_kv2, tm, LANES), row3),
                   pl.BlockSpec((n_kv2, tm, LANES), row3),
                   pl.BlockSpec((tm, QI_COLS), row),
                   pl.BlockSpec((2, tm, LANES), row3),
                   pl.BlockSpec((tm, LANES), row)],
        out_shape=[jax.ShapeDtypeStruct((m, Q_COLS), BF16),
                   jax.ShapeDtypeStruct((m, KV_COLS), F32),
                   jax.ShapeDtypeStruct((m, KV_COLS), F32),
                   jax.ShapeDtypeStruct((n_kv2, m, LANES), BF16),
                   jax.ShapeDtypeStruct((n_kv2, m, LANES), BF16),
                   jax.ShapeDtypeStruct((m, QI_COLS), BF16),
                   jax.ShapeDtypeStruct((2, m, LANES), BF16),
                   jax.ShapeDtypeStruct((m, LANES), F32)],
        compiler_params=_params(1),
    )(x, w_pad, *tables)


def _order_key(score):
    score = jnp.where(score == 0.0, 0.0, score)
    bits = pltpu.bitcast(score, jnp.int32)
    return bits ^ ((bits >> 31) & 0x7FFFFFFF)


def _kth_largest(count_ge, shape, k):
    def body(it, t):
        cand = t + lax.shift_left(jnp.int32(1), 31 - it)
        return jnp.where(count_ge(cand) >= k, cand, t)
    return lax.fori_loop(0, 32, body, jnp.full(shape, INT_MIN, jnp.int32))


def _dsa_prompt_kernel(q_ref, qi_ref, kiwq_ref, kiab_ref, kab_ref, vab_ref, x_ref, wo_ref,
                       g_ref, b_ref, o_ref, key_ref, acc_ref, m_ref, *, tq, topk, alpha):
    i = pl.program_id(1)
    n_chunks = i + 1
    seq = key_ref.shape[1]
    row = lax.broadcasted_iota(jnp.int32, (tq, tq), 0)
    col = lax.broadcasted_iota(jnp.int32, (tq, tq), 1)
    lane_tiles = tq // LANES

    def chunk_off(j):
        return pl.multiple_of(j * tq, tq)

    w_idx = kiwq_ref[:, HEAD_DIM:HEAD_DIM + IDX_HEADS] * (IDX_DIM ** -0.5 * IDX_HEADS ** -0.5)

    def score_chunk(j, carry):
        off = chunk_off(j)
        sc = jnp.zeros((tq, tq), F32)
        for h in range(IDX_HEADS):
            qt = qi_ref[:, (h // 2) * LANES:(h // 2 + 1) * LANES]
            s = lax.dot_general(qt, kiab_ref[h % 2, pl.ds(off, tq), :], _NT,
                                preferred_element_type=F32)
            sc = sc + jnp.maximum(s, 0.0) * w_idx[:, h:h + 1]
        key = _order_key(sc)
        key = jnp.where(jnp.logical_and(j == i, col > row), INT_MIN, key)
        key_ref[:, pl.ds(off, tq)] = key
        return carry

    lax.fori_loop(0, n_chunks, score_chunk, 0)

    def count_rows(pred):
        def body(j, cnt):
            off = chunk_off(j)
            ind = jnp.where(pred(key_ref[:, pl.ds(off, tq)], off), 1, 0)
            for t in range(lane_tiles):
                cnt = cnt + ind[:, t * LANES:(t + 1) * LANES]
            return cnt
        cnt = lax.fori_loop(0, n_chunks, body, jnp.zeros((tq, LANES), jnp.int32))
        return jnp.sum(cnt, axis=1, keepdims=True)

    thr = _kth_largest(lambda cand: count_rows(lambda x, off: x >= cand), (tq, 1), topk)
    n_gt = count_rows(lambda x, off: x > thr)
    n_ge = count_rows(lambda x, off: x >= thr)
    tie = jnp.logical_and(n_ge > topk, thr > INT_MIN)

    @pl.when(jnp.max(jnp.where(tie, 1, 0)) > 0)
    def _():
        need = topk - n_gt
        n_bits = (seq - 1).bit_length()

        def pbody(it, p):
            cand = p + lax.shift_left(jnp.int32(1), n_bits - 1 - it)
            c = count_rows(lambda x, off: jnp.logical_and(x == thr, (col + off) < cand))
            return jnp.where(c < need, cand, p)
        last = lax.fori_loop(0, n_bits, pbody, jnp.zeros((tq, 1), jnp.int32))

        def patch(j, carry):
            off = chunk_off(j)
            x = key_ref[:, pl.ds(off, tq)]
            drop = jnp.logical_and(x == thr, (col + off) > last)
            key_ref[:, pl.ds(off, tq)] = jnp.where(jnp.logical_and(tie, drop), thr - 1, x)
            return carry
        lax.fori_loop(0, n_chunks, patch, 0)

    thr = jnp.maximum(thr, INT_MIN + 1)

    acc_ref[...] = jnp.zeros_like(acc_ref)
    m_ref[...] = jnp.full_like(m_ref, NEG_BIG)

    def attn_chunk(j, carry):
        off = chunk_off(j)
        sel = key_ref[:, pl.ds(off, tq)] >= thr
        for h in range(N_HEADS):
            kv = 2 * (h // GQA_GROUP) + h % 2
            qt = q_ref[:, (h // 2) * LANES:(h // 2 + 1) * LANES]
            s = lax.dot_general(qt, kab_ref[kv, pl.ds(off, tq), :], _NT, preferred_element_type=F32)
            s = jnp.where(sel, s, NEG_BIG)
            m_prev = m_ref[h]
            m_next = jnp.maximum(m_prev, jnp.max(s, axis=1, keepdims=True))
            p = jnp.exp(s - jnp.concatenate([m_next] * lane_tiles, axis=1))
            acc_ref[h] = (jnp.exp(m_prev - m_next) * acc_ref[h]
                          + jnp.dot(p.astype(BF16), vab_ref[kv, pl.ds(off, tq), :],
                                    preferred_element_type=F32))
            m_ref[h] = m_next
        return carry

    lax.fori_loop(0, n_chunks, attn_chunk, 0)

    low = lax.broadcasted_iota(jnp.int32, (tq, LANES), 1) < HEAD_DIM
    tiles = []
    for t in range(N_HEADS // 2):
        a0 = acc_ref[2 * t]
        a1 = acc_ref[2 * t + 1]
        o0 = a0 / a0[:, HEAD_DIM:HEAD_DIM + 1]
        o1 = a1 / a1[:, 0:1]
        tiles.append(jnp.where(low, o0, o1).astype(BF16))
    attn = jnp.concatenate(tiles, axis=1)
    mix = jnp.dot(attn, wo_ref[...], preferred_element_type=F32)
    o_ref[...] = _layernorm(alpha * x_ref[...] + mix, g_ref[...], b_ref[...])


def _dsa_prompt(q, qi, kiw, kiab, kab, vab, x, wo, g, b, *, batch, seq, tq, alpha):
    m, d = x.shape
    assert seq % tq == 0 and tq % LANES == 0
    nq = seq // tq
    topk = min(TOPK_MAX, seq // 4)
    row = lambda bi, i: (bi * nq + i, 0)
    per_seq = lambda bi, i: (0, bi, 0)
    n_kv2 = 2 * N_KV_HEADS
    seq_spec = lambda n: pl.BlockSpec((n, seq, LANES), per_seq, pipeline_mode=pl.Buffered(1))
    return pl.pallas_call(
        functools.partial(_dsa_prompt_kernel, tq=tq, topk=topk, alpha=alpha),
        grid=(batch, nq),
        in_specs=[pl.BlockSpec((tq, Q_COLS), row),
                  pl.BlockSpec((tq, QI_COLS), row),
                  pl.BlockSpec((tq, LANES), row),
                  seq_spec(2), seq_spec(n_kv2), seq_spec(n_kv2),
                  pl.BlockSpec((tq, d), row),
                  _const_spec(wo.shape), _const_spec((1, d)), _const_spec((1, d))],
        out_specs=pl.BlockSpec((tq, d), row),
        out_shape=jax.ShapeDtypeStruct((m, d), F32),
        scratch_shapes=[pltpu.VMEM((tq, seq), jnp.int32),
                        pltpu.VMEM((N_HEADS, tq, LANES), F32),
                        pltpu.VMEM((N_HEADS, tq, LANES), F32)],
        compiler_params=_params(2),
    )(q, qi, kiw, kiab, kab, vab, x, wo, g.reshape(1, d), b.reshape(1, d))


PAGES_PER_CHUNK = 16
PAIRS = PAGE_SIZE // 2


def _dsa_sample_kernel(pt_ref, wqi_ref, widx_ref, qi2_ref, kinew_ref, wq_ref, q2_ref, knew_ref,
                       vself_ref, kidx_hbm, k_hbm, v_hbm, o_ref,
                       kib_ref, buf_ref, s_ref, sem_ki, sem_kv, *, n_pages, topk):
    b = pl.program_id(0)
    n_pairs = n_pages * PAIRS
    n_chunks = n_pages // PAGES_PER_CHUNK
    chunk_rows = PAGES_PER_CHUNK * PAIRS

    def ki_copy(pg):
        return pltpu.make_async_copy(kidx_hbm.at[pt_ref[b, pg]],
                                     kib_ref.at[pl.ds(pg * PAIRS, PAIRS), :], sem_ki)

    def kv_copy(src_hbm, step, pg):
        slot = step % 2
        page = (step % n_chunks) * PAGES_PER_CHUNK + pg
        return pltpu.make_async_copy(src_hbm.at[pt_ref[b, page]],
                                     buf_ref.at[slot, pl.ds(pg * PAIRS, PAIRS), :], sem_kv.at[slot])

    def start_chunk(step):
        src = k_hbm if step < n_chunks else v_hbm
        for pg in range(PAGES_PER_CHUNK):
            kv_copy(src, step, pg).start()

    def wait_chunk(step):
        src = k_hbm if step < n_chunks else v_hbm
        for pg in range(PAGES_PER_CHUNK):
            kv_copy(src, step, pg).wait()

    def ki_start(pg, c):
        ki_copy(pg).start()
        return c

    def ki_wait(pg, c):
        ki_copy(pg).wait()
        return c

    lax.fori_loop(0, n_pages, ki_start, 0)
    start_chunk(0)
    lax.fori_loop(0, n_pages, ki_wait, 0)

    w_idx = widx_ref[0]
    s_idx = lax.dot_general(wqi_ref[0], kib_ref[...].astype(BF16), _NT, preferred_element_type=F32)
    weighted = jnp.maximum(s_idx, 0.0) * w_idx
    sc = jnp.concatenate(
        [jnp.sum(weighted[e * IDX_HEADS:(e + 1) * IDX_HEADS], axis=0, keepdims=True) for e in range(2)],
        axis=0)
    key = _order_key(sc)
    s_self_idx = jnp.sum(qi2_ref[0].astype(F32) * kinew_ref[0], axis=1, keepdims=True)
    sc_self = jnp.sum(jnp.maximum(s_self_idx, 0.0) * w_idx[:IDX_HEADS], axis=0, keepdims=True)
    key_self = _order_key(sc_self)
    parity = lax.broadcasted_iota(jnp.int32, (2, n_pairs), 0)
    pos = 2 * lax.broadcasted_iota(jnp.int32, (2, n_pairs), 1) + parity
    pos_self = 2 * n_pairs

    def count(pred_past, pred_self):
        c = jnp.sum(jnp.where(pred_past, 1, 0), axis=1, keepdims=True)
        return jnp.sum(c, axis=0, keepdims=True) + jnp.where(pred_self, 1, 0)

    thr = _kth_largest(lambda cand: count(key >= cand, key_self >= cand), (1, 1), topk)
    n_gt = count(key > thr, key_self > thr)
    n_ge = count(key >= thr, key_self >= thr)
    need = topk - n_gt
    n_bits = pos_self.bit_length()

    def pbody(it, p):
        cand = p + lax.shift_left(jnp.int32(1), n_bits - 1 - it)
        c = count(jnp.logical_and(key == thr, pos < cand),
                  jnp.logical_and(key_self == thr, pos_self < cand))
        return jnp.where(c < need, cand, p)
    last = lax.fori_loop(0, n_bits, pbody, jnp.zeros((1, 1), jnp.int32))
    last = jnp.where(n_ge > topk, last, pos_self)
    sel = jnp.logical_or(key > thr, jnp.logical_and(key == thr, pos <= last))
    sel_self = jnp.logical_or(key_self > thr, jnp.logical_and(key_self == thr, pos_self <= last))

    wq = wq_ref[0]
    for step in range(n_chunks):
        start_chunk(step + 1)
        wait_chunk(step)
        kc = buf_ref[step % 2].astype(BF16)
        s_ref[:, step * chunk_rows:(step + 1) * chunk_rows] = lax.dot_general(
            wq, kc, _NT, preferred_element_type=F32)

    s_self = jnp.sum(q2_ref[0].astype(F32) * knew_ref[0], axis=1, keepdims=True)
    rows_par = lax.broadcasted_iota(jnp.int32, (2 * N_HEADS, n_pairs), 0) >= N_HEADS
    sel_i = jnp.where(sel, 1, 0)
    sel_rows = jnp.where(rows_par, sel_i[1:2], sel_i[0:1]) > 0
    s_all = jnp.where(sel_rows, s_ref[...], NEG_BIG)
    m_rows = jnp.max(s_all, axis=1, keepdims=True)
    s_self = jnp.where(sel_self, s_self, NEG_BIG)
    m_head = jnp.maximum(jnp.maximum(m_rows[:N_HEADS], m_rows[N_HEADS:]), s_self)
    p = jnp.where(sel_rows, jnp.exp(s_all - jnp.concatenate([m_head, m_head], axis=0)), 0.0)
    p_self = jnp.where(sel_self, jnp.exp(s_self - m_head), 0.0)
    pb = p.astype(BF16)
    l_rows = jnp.sum(pb.astype(F32), axis=1, keepdims=True)
    l_head = l_rows[:N_HEADS] + l_rows[N_HEADS:] + p_self

    acc = jnp.zeros((2 * N_HEADS, 2 * KV_COLS), F32)
    for step in range(n_chunks, 2 * n_chunks):
        if step + 1 < 2 * n_chunks:
            start_chunk(step + 1)
        wait_chunk(step)
        c = step - n_chunks
        vc = buf_ref[step % 2].astype(BF16)
        acc = acc + jnp.dot(pb[:, c * chunk_rows:(c + 1) * chunk_rows], vc, preferred_element_type=F32)

    lane_grp = lax.broadcasted_iota(jnp.int32, (N_HEADS, KV_COLS), 1) // HEAD_DIM
    head_grp = lax.broadcasted_iota(jnp.int32, (N_HEADS, KV_COLS), 0) // GQA_GROUP
    mine = lane_grp == head_grp
    out = (jnp.where(mine, acc[:N_HEADS, :KV_COLS], 0.0) + jnp.where(mine, acc[N_HEADS:, KV_COLS:], 0.0)
           + p_self * vself_ref[0]) / l_head
    folded = out[0:GQA_GROUP]
    for g in range(1, N_KV_HEADS):
        folded = folded + out[g * GQA_GROUP:(g + 1) * GQA_GROUP]
    o_ref[0] = folded


def _dsa_sample(q, qi, kiw, k_new, v_new, cache_k, cache_v, cache_kidx, page_table):
    bs = q.shape[0]
    n_pages = page_table.shape[1]
    assert n_pages % PAGES_PER_CHUNK == 0
    n_pairs = n_pages * PAIRS
    topk = min(TOPK_MAX, (n_pages * PAGE_SIZE + 1) // 4)
    n_phys = cache_k.shape[0]
    kidx_v = cache_kidx.reshape(n_phys, PAIRS, 2 * IDX_DIM)
    k_v = cache_k.reshape(n_phys, PAIRS, 2 * KV_COLS)
    v_v = cache_v.reshape(n_phys, PAIRS, 2 * KV_COLS)

    eye2 = jnp.eye(2, dtype=BF16)
    qi_h = qi.reshape(bs, IDX_HEADS, IDX_DIM)
    wqi = (eye2[None, :, None, :, None] * qi_h[:, None, :, None, :]).reshape(bs, 2 * IDX_HEADS, 2 * IDX_DIM)
    w_idx = kiw[:, HEAD_DIM:HEAD_DIM + IDX_HEADS] * (IDX_DIM ** -0.5 * IDX_HEADS ** -0.5)
    w_idx = jnp.tile(w_idx, (1, 2)).reshape(bs, 2 * IDX_HEADS, 1)
    ki_new = jnp.broadcast_to(kiw[:, None, :IDX_DIM], (bs, IDX_HEADS, IDX_DIM))
    q_h = q.reshape(bs, N_HEADS, HEAD_DIM)
    grp = jnp.arange(N_HEADS) // GQA_GROUP
    grp_onehot = (grp[:, None] == jnp.arange(N_KV_HEADS)[None, :])
    wq = (eye2[None, :, None, :, None, None] * grp_onehot.astype(BF16)[None, None, :, None, :, None]
          * q_h[:, None, :, None, None, :]).reshape(bs, 2 * N_HEADS, 2 * KV_COLS)
    k_rep = jnp.repeat(k_new.reshape(bs, N_KV_HEADS, HEAD_DIM), GQA_GROUP, axis=1)
    v_grp = v_new.reshape(bs, N_KV_HEADS, HEAD_DIM)
    vself = (grp_onehot.astype(F32)[None, :, :, None] * v_grp[:, None, :, :]).reshape(bs, N_HEADS, KV_COLS)

    per_b = lambda shape: pl.BlockSpec((1,) + shape, lambda bi, pt: (bi, 0, 0))
    any_spec = pl.BlockSpec(memory_space=pl.ANY)
    grid_spec = pltpu.PrefetchScalarGridSpec(
        num_scalar_prefetch=1,
        grid=(bs,),
        in_specs=[per_b((2 * IDX_HEADS, 2 * IDX_DIM)), per_b((2 * IDX_HEADS, 1)),
                  per_b((IDX_HEADS, IDX_DIM)), per_b((IDX_HEADS, IDX_DIM)),
                  per_b((2 * N_HEADS, 2 * KV_COLS)), per_b((N_HEADS, HEAD_DIM)),
                  per_b((N_HEADS, HEAD_DIM)), per_b((N_HEADS, KV_COLS)),
                  any_spec, any_spec, any_spec],
        out_specs=per_b((GQA_GROUP, KV_COLS)),
        scratch_shapes=[pltpu.VMEM((n_pairs, 2 * IDX_DIM), F32),
                        pltpu.VMEM((2, PAGES_PER_CHUNK * PAIRS, 2 * KV_COLS), F32),
                        pltpu.VMEM((2 * N_HEADS, n_pairs), F32),
                        pltpu.SemaphoreType.DMA(()),
                        pltpu.SemaphoreType.DMA((2,))])
    out = pl.pallas_call(
        functools.partial(_dsa_sample_kernel, n_pages=n_pages, topk=topk),
        grid_spec=grid_spec,
        out_shape=jax.ShapeDtypeStruct((bs, GQA_GROUP, KV_COLS), F32),
        compiler_params=_params(1),
    )(page_table, wqi, w_idx, qi_h, ki_new, wq, q_h, k_rep, vself, kidx_v, k_v, v_v)
    return out.reshape(bs, Q_COLS)


def _matmul_ln_kernel(a_ref, w_ref, x_ref, g_ref, b_ref, o_ref, *, alpha):
    mix = jnp.dot(a_ref[...].astype(BF16), w_ref[...], preferred_element_type=F32)
    o_ref[...] = _layernorm(alpha * x_ref[...] + mix, g_ref[...], b_ref[...])


def _matmul_ln(a, w, x, g, b, *, alpha):
    m, d = x.shape
    return pl.pallas_call(
        functools.partial(_matmul_ln_kernel, alpha=alpha),
        out_shape=jax.ShapeDtypeStruct((m, d), F32),
        compiler_params=pltpu.CompilerParams(vmem_limit_bytes=VMEM_LIMIT_BYTES),
    )(a, w, x, g.reshape(1, d), b.reshape(1, d))


HALO = 16


def _pool_mix(diffs, wp_ref, scale):
    outs = [jnp.dot(d.astype(BF16), wp_ref[g], preferred_element_type=F32) for g, d in enumerate(diffs)]
    return jnp.concatenate(outs, axis=1) * scale


def _pool_prompt_kernel(x_ref, halo_ref, wp_ref, sc_ref, g_ref, b_ref, o_ref, ext_ref,
                        *, tm, tiles_per_seq, alpha):
    i = pl.program_id(0)
    t_in_seq = i % tiles_per_seq
    x = x_ref[...]
    ext_ref[HALO:, :] = x
    ext_ref[:HALO, :] = jnp.where(t_in_seq == 0, 0.0, halo_ref[...])
    pos = t_in_seq * tm + lax.broadcasted_iota(jnp.int32, (tm, 1), 0)
    pool_ch = x.shape[1] // len(POOL_WINDOWS)
    diffs = []
    for g, w in enumerate(POOL_WINDOWS):
        ch = slice(g * pool_ch, (g + 1) * pool_ch)
        tot = x[:, ch]
        for s in range(1, w):
            tot = tot + ext_ref[HALO - s:HALO - s + tm, ch]
        cnt = jnp.minimum(pos + 1, w).astype(F32)
        diffs.append(tot / cnt - x[:, ch])
    mix = _pool_mix(diffs, wp_ref, sc_ref[...])
    o_ref[...] = _layernorm(alpha * x + mix, g_ref[...], b_ref[...])


def _pool_prompt(x, wp, scale, g, b, *, seq, tm, alpha):
    m, d = x.shape
    assert seq % tm == 0 and tm % HALO == 0
    tiles_per_seq = seq // tm
    halo_blocks = tm // HALO
    return pl.pallas_call(
        functools.partial(_pool_prompt_kernel, tm=tm, tiles_per_seq=tiles_per_seq, alpha=alpha),
        grid=(m // tm,),
        in_specs=[pl.BlockSpec((tm, d), lambda i: (i, 0)),
                  pl.BlockSpec((HALO, d), lambda i: (jnp.maximum(i * halo_blocks - 1, 0), 0)),
                  _const_spec(wp.shape), _const_spec((1, d)), _const_spec((1, d)), _const_spec((1, d))],
        out_specs=pl.BlockSpec((tm, d), lambda i: (i, 0)),
        out_shape=jax.ShapeDtypeStruct((m, d), F32),
        scratch_shapes=[pltpu.VMEM((tm + HALO, d), F32)],
        compiler_params=_params(1),
    )(x, x, wp, scale.reshape(1, d), g.reshape(1, d), b.reshape(1, d))


def _pool_sample_kernel(ext_ref, wp_ref, sc_ref, g_ref, b_ref, o_ref, *, alpha):
    n_rows = ext_ref.shape[1]
    x = ext_ref[:, n_rows - 1, :]
    pool_ch = x.shape[1] // len(POOL_WINDOWS)
    diffs = []
    for g, w in enumerate(POOL_WINDOWS):
        ch = slice(g * pool_ch, (g + 1) * pool_ch)
        tot = x[:, ch]
        for s in range(1, w):
            tot = tot + ext_ref[:, n_rows - 1 - s, ch]
        diffs.append(tot / float(w) - x[:, ch])
    mix = _pool_mix(diffs, wp_ref, sc_ref[...])
    o_ref[...] = _layernorm(alpha * x + mix, g_ref[...], b_ref[...])


def _pool_sample(ext, wp, scale, g, b, *, alpha):
    bs, n_rows, d = ext.shape
    assert n_rows >= max(POOL_WINDOWS)
    return pl.pallas_call(
        functools.partial(_pool_sample_kernel, alpha=alpha),
        out_shape=jax.ShapeDtypeStruct((bs, d), F32),
        compiler_params=pltpu.CompilerParams(vmem_limit_bytes=VMEM_LIMIT_BYTES),
    )(ext, wp, scale.reshape(1, d), g.reshape(1, d), b.reshape(1, d))


def _row_tile(m, target):
    return target if m % target == 0 else m


def kernel(x_prompt, x_sample, cache_k, cache_v, cache_kidx, state_pool, page_table, ln_g, ln_b,
           ffn1_wi, ffn1_wo, ffn2_wi, ffn2_wo, attn_w_in, attn_w_o, pool_w, pool_scale):
    batch, seq, d = x_prompt.shape
    bs, dec_seq, _ = x_sample.shape
    assert dec_seq == 1
    depth = ln_g.shape[0]
    alpha = (2 * depth) ** 0.25
    n_pages = page_table.shape[1]
    past = n_pages * PAGE_SIZE
    n_phys = cache_k.shape[1]

    xp = x_prompt.reshape(batch * seq, d)
    xs = x_sample.reshape(bs, d)
    tm_p = _row_tile(batch * seq, 512)
    tq = _row_tile(seq, 256)

    def ffn(x, wi, wo, g, b, tm):
        return _ffn_ln(x, wi.astype(BF16), wo.astype(BF16), g, b, tm=tm, alpha=alpha)

    tabs_p = _rope_tables(jnp.arange(seq, dtype=jnp.int32).astype(F32))
    tabs_s = _rope_tables((past + jnp.zeros((bs,), jnp.int32)).astype(F32))

    kp_l, vp_l, kip_l, poolp_l = [], [], [], []
    ks_l, vs_l, kis_l, pools_l = [], [], [], []
    for l in range(depth):
        j = l // N_MIXERS
        xp = ffn(xp, ffn1_wi[l], ffn1_wo[l], ln_g[l, 0], ln_b[l, 0], tm_p)
        xs = ffn(xs, ffn1_wi[l], ffn1_wo[l], ln_g[l, 0], ln_b[l, 0], bs)
        if l % N_MIXERS == 0:
            w_in = attn_w_in[j].astype(BF16)
            w_in = jnp.pad(w_in, ((0, 0), (0, W_IN_PAD - w_in.shape[1])))
            w_o = attn_w_o[j].astype(BF16)
            q, k, v, kab, vab, qi, kiab, kiw = _attn_proj(xp, w_in, tabs_p, tm=tm_p)
            xp = _dsa_prompt(q, qi, kiw, kiab, kab, vab, xp, w_o, ln_g[l, 1], ln_b[l, 1],
                             batch=batch, seq=seq, tq=tq, alpha=alpha)
            kp_l.append(k.reshape(batch, seq, N_KV_HEADS, HEAD_DIM))
            vp_l.append(v.reshape(batch, seq, N_KV_HEADS, HEAD_DIM))
            kip_l.append(kiw[:, :IDX_DIM].reshape(batch, seq, IDX_DIM))

            qs, k_s, v_s, _, _, qis, _, kiws = _attn_proj(xs, w_in, tabs_s, tm=bs)
            attn_s = _dsa_sample(qs, qis, kiws, k_s, v_s,
                                 cache_k[j].reshape(n_phys, PAGE_SIZE, KV_COLS),
                                 cache_v[j].reshape(n_phys, PAGE_SIZE, KV_COLS),
                                 cache_kidx[j], page_table)
            w_o_perm = w_o.reshape(N_KV_HEADS, GQA_GROUP, HEAD_DIM, d).transpose(1, 0, 2, 3).reshape(Q_COLS, d)
            xs = _matmul_ln(attn_s, w_o_perm, xs, ln_g[l, 1], ln_b[l, 1], alpha=alpha)
            ks_l.append(k_s.reshape(bs, 1, N_KV_HEADS, HEAD_DIM))
            vs_l.append(v_s.reshape(bs, 1, N_KV_HEADS, HEAD_DIM))
            kis_l.append(kiws[:, :IDX_DIM].reshape(bs, 1, IDX_DIM))
        else:
            wp = pool_w[j].astype(BF16)
            poolp_l.append(xp.reshape(batch, seq, d)[:, seq - POOL_BUF:])
            ext = jnp.concatenate([state_pool[j], xs[:, None, :]], axis=1)
            pools_l.append(ext[:, ext.shape[1] - POOL_BUF:])
            xp = _pool_prompt(xp, wp, pool_scale[j], ln_g[l, 1], ln_b[l, 1],
                              seq=seq, tm=_row_tile(seq, 512), alpha=alpha)
            xs = _pool_sample(ext, wp, pool_scale[j], ln_g[l, 1], ln_b[l, 1], alpha=alpha)
        xp = ffn(xp, ffn2_wi[l], ffn2_wo[l], ln_g[l, 2], ln_b[l, 2], tm_p)
        xs = ffn(xs, ffn2_wi[l], ffn2_wo[l], ln_g[l, 2], ln_b[l, 2], bs)

    return (xp.reshape(batch, seq, d), xs.reshape(bs, 1, d),
            jnp.stack(kp_l), jnp.stack(vp_l), jnp.stack(kip_l), jnp.stack(poolp_l),
            jnp.stack(ks_l), jnp.stack(vs_l), jnp.stack(kis_l), jnp.stack(pools_l))
```

```python
import functools

import jax
import jax.numpy as jnp
from jax import lax
from jax.experimental import pallas as pl
from jax.experimental.pallas import tpu as pltpu

N_MIXERS = 2
N_HEADS = 16
HEAD_DIM = 64
N_KV_HEADS = 4
GQA_GROUP = N_HEADS // N_KV_HEADS
ROT_DIM = HEAD_DIM // 4
ROPE_THETA = 500000.0
IDX_HEADS = 8
IDX_DIM = 64
TOPK_MAX = 256
PAGE_SIZE = 128
POOL_WINDOWS = (2, 4, 8, 16)
POOL_BUF = max(POOL_WINDOWS) - 1
LN_EPS = 1e-5
Q_COLS = N_HEADS * HEAD_DIM
KV_COLS = N_KV_HEADS * HEAD_DIM
QI_COLS = IDX_HEADS * IDX_DIM

LANES = 128
SUBLANES = 8
BF16_ROWS = 16
VMEM_LIMIT_BYTES = 56 * 1024 * 1024

F32 = jnp.float32
BF16 = jnp.bfloat16
INT_MIN = -(2 ** 31)
NEG_BIG = -1e30

_NT = (((1,), (1,)), ((), ()))


def _layernorm(y, g, b):
    mu = jnp.mean(y, axis=-1, keepdims=True)
    yc = y - mu
    var = jnp.mean(yc * yc, axis=-1, keepdims=True)
    return yc * lax.rsqrt(var + LN_EPS) * g + b


def _const_spec(shape):
    nd = len(shape)
    return pl.BlockSpec(shape, lambda *_: (0,) * nd, pipeline_mode=pl.Buffered(1))


def _params(n_grid):
    return pltpu.CompilerParams(dimension_semantics=("arbitrary",) * n_grid,
                                vmem_limit_bytes=VMEM_LIMIT_BYTES)


FFN_CHUNK = 256


def _ffn_kernel(x_ref, wi_ref, wo_ref, g_ref, b_ref, o_ref, acc_ref, *, d_ff, alpha):
    x = x_ref[...]
    xb = x.astype(BF16)
    for c in range(d_ff // FFN_CHUNK):
        lo = c * FFN_CHUNK
        hg = jnp.dot(xb, wi_ref[:, lo:lo + FFN_CHUNK], preferred_element_type=F32)
        hu = jnp.dot(xb, wi_ref[:, d_ff + lo:d_ff + lo + FFN_CHUNK], preferred_element_type=F32)
        a = (hg * jax.nn.sigmoid(hg) * hu).astype(BF16)
        part = jnp.dot(a, wo_ref[lo:lo + FFN_CHUNK, :], preferred_element_type=F32)
        if c == 0:
            acc_ref[...] = part
        else:
            acc_ref[...] += part
    o_ref[...] = _layernorm(alpha * x + 0.5 * acc_ref[...], g_ref[...], b_ref[...])


def _ffn_ln(x, wi, wo, g, b, *, tm, alpha):
    m, d = x.shape
    d_ff = wo.shape[0]
    assert m % tm == 0 and d_ff % FFN_CHUNK == 0
    return pl.pallas_call(
        functools.partial(_ffn_kernel, d_ff=d_ff, alpha=alpha),
        grid=(m // tm,),
        in_specs=[pl.BlockSpec((tm, d), lambda i: (i, 0)),
                  _const_spec(wi.shape), _const_spec(wo.shape),
                  _const_spec((1, d)), _const_spec((1, d))],
        out_specs=pl.BlockSpec((tm, d), lambda i: (i, 0)),
        out_shape=jax.ShapeDtypeStruct((m, d), F32),
        scratch_shapes=[pltpu.VMEM((tm, d), F32)],
        compiler_params=_params(1),
    )(x, wi, wo, g.reshape(1, d), b.reshape(1, d))


O_K = Q_COLS
O_V = O_K + KV_COLS
O_QI = O_V + KV_COLS
O_KI = O_QI + QI_COLS
W_IN_PAD = O_KI + LANES
LOG2E = 1.4426950408889634
Q_SCALE = HEAD_DIM ** -0.5 * LOG2E


def _proj_kernel(x_ref, w_ref, cos_ref, sa_ref, sb_ref, q_ref, k_ref, v_ref, qi_ref, kiw_ref,
                 *attn_refs):
    xb = x_ref[...].astype(BF16)
    cos = cos_ref[...]
    sin_a = sa_ref[...]
    sin_b = sb_ref[...]
    tm = xb.shape[0]
    low = lax.broadcasted_iota(jnp.int32, (tm, LANES), 1) < HEAD_DIM

    def rope(t):
        return (t * cos + pltpu.roll(t, ROT_DIM // 2, 1) * sin_a
                + pltpu.roll(t, LANES - ROT_DIM // 2, 1) * sin_b)

    def proj(lo, width):
        return jnp.dot(xb, w_ref[:, lo:lo + width], preferred_element_type=F32)

    pq = proj(0, Q_COLS)
    for t in range(Q_COLS // LANES):
        q_ref[:, t * LANES:(t + 1) * LANES] = (
            rope(pq[:, t * LANES:(t + 1) * LANES]) * Q_SCALE).astype(BF16)
    pqi = proj(O_QI, QI_COLS)
    for t in range(QI_COLS // LANES):
        qi_ref[:, t * LANES:(t + 1) * LANES] = rope(pqi[:, t * LANES:(t + 1) * LANES]).astype(BF16)
    pkw = proj(O_KI, LANES)
    kw = jnp.where(low, rope(pkw), pkw)
    kiw_ref[...] = kw

    pk = proj(O_K, KV_COLS)
    pv = proj(O_V, KV_COLS)
    if attn_refs:
        kab_ref, vab_ref, kiab_ref = attn_refs
        kiab_ref[0] = jnp.where(low, kw, 0.0).astype(BF16)
        kiab_ref[1] = jnp.where(low, 0.0, pltpu.roll(kw, HEAD_DIM, 1)).astype(BF16)
        lane = lax.broadcasted_iota(jnp.int32, (tm, LANES), 1)
        ones_a = jnp.where(lane == HEAD_DIM, 1.0, 0.0)
        ones_b = jnp.where(lane == 0, 1.0, 0.0)
    for t in range(KV_COLS // LANES):
        kr = rope(pk[:, t * LANES:(t + 1) * LANES])
        k_ref[:, t * LANES:(t + 1) * LANES] = kr
        vv = pv[:, t * LANES:(t + 1) * LANES]
        v_ref[:, t * LANES:(t + 1) * LANES] = vv
        if attn_refs:
            krr = pltpu.roll(kr, HEAD_DIM, 1)
            g0, g1 = 2 * t, 2 * t + 1
            kab_ref[2 * g0] = jnp.where(low, kr, 0.0).astype(BF16)
            kab_ref[2 * g0 + 1] = jnp.where(low, 0.0, krr).astype(BF16)
            kab_ref[2 * g1] = jnp.where(low, krr, 0.0).astype(BF16)
            kab_ref[2 * g1 + 1] = jnp.where(low, 0.0, kr).astype(BF16)
            vvr = pltpu.roll(vv, HEAD_DIM, 1)
            vab_ref[2 * g0] = jnp.where(low, vv, ones_a).astype(BF16)
            vab_ref[2 * g0 + 1] = jnp.where(low, ones_b, vvr).astype(BF16)
            vab_ref[2 * g1] = jnp.where(low, vvr, ones_a).astype(BF16)
            vab_ref[2 * g1 + 1] = jnp.where(low, ones_b, vv).astype(BF16)


def _rope_tables(pos):
    half = ROT_DIM // 2
    freqs = ROPE_THETA ** (-jnp.arange(half, dtype=F32) / half)
    ang = pos[:, None] * freqs[None, :]
    cos, sin = jnp.cos(ang), jnp.sin(ang)
    t = pos.shape[0]
    rest = HEAD_DIM - ROT_DIM
    zeros_h = jnp.zeros((t, half), F32)
    c64 = jnp.concatenate([cos, cos, jnp.ones((t, rest), F32)], axis=1)
    a64 = jnp.concatenate([zeros_h, sin, jnp.zeros((t, rest), F32)], axis=1)
    b64 = jnp.concatenate([-sin, zeros_h, jnp.zeros((t, rest), F32)], axis=1)
    rep = LANES // HEAD_DIM
    return tuple(jnp.tile(a, (1, rep)) for a in (c64, a64, b64))


def _attn_proj(x, w_pad, tables, *, tm, attn_layouts):
    m, d = x.shape
    t_rows = tables[0].shape[0]
    assert m % tm == 0 and t_rows % tm == 0
    n_t = t_rows // tm
    row = lambda i: (i, 0)
    tab_spec = pl.BlockSpec((tm, LANES), lambda i: (i % n_t, 0))
    n_kv2 = 2 * N_KV_HEADS
    out_specs = [pl.BlockSpec((tm, Q_COLS), row), pl.BlockSpec((tm, KV_COLS), row),
                 pl.BlockSpec((tm, KV_COLS), row), pl.BlockSpec((tm, QI_COLS), row),
                 pl.BlockSpec((tm, LANES), row)]
    out_shape = [jax.ShapeDtypeStruct((m, Q_COLS), BF16), jax.ShapeDtypeStruct((m, KV_COLS), F32),
                 jax.ShapeDtypeStruct((m, KV_COLS), F32), jax.ShapeDtypeStruct((m, QI_COLS), BF16),
                 jax.ShapeDtypeStruct((m, LANES), F32)]
    if attn_layouts:
        out_specs += [pl.BlockSpec((n_kv2, tm, LANES), lambda i: (0, i, 0)),
                      pl.BlockSpec((n_kv2, tm, LANES), lambda i: (0, i, 0)),
                      pl.BlockSpec((2, tm, LANES), lambda i: (0, i, 0))]
        out_shape += [jax.ShapeDtypeStruct((n_kv2, m, LANES), BF16),
                      jax.ShapeDtypeStruct((n_kv2, m, LANES), BF16),
                      jax.ShapeDtypeStruct((2, m, LANES), BF16)]
    return pl.pallas_call(
        _proj_kernel,
        grid=(m // tm,),
        in_specs=[pl.BlockSpec((tm, d), row), _const_spec(w_pad.shape), tab_spec, tab_spec, tab_spec],
        out_specs=out_specs,
        out_shape=out_shape,
        compiler_params=_params(1),
    )(x, w_pad, *tables)


def _order_key(score):
    score = jnp.where(score == 0.0, 0.0, score)
    bits = pltpu.bitcast(score, jnp.int32)
    return bits ^ ((bits >> 31) & 0x7FFFFFFF)


def _kth_largest(count_ge, shape, k):
    def body(it, t):
        cand = t + lax.shift_left(jnp.int32(1), 31 - it)
        return jnp.where(count_ge(cand) >= k, cand, t)
    return lax.fori_loop(0, 32, body, jnp.full(shape, INT_MIN, jnp.int32))


def _dsa_prompt_kernel(q_ref, qi_ref, kiwq_ref, kiab_ref, kab_ref, vab_ref, x_ref, wo_ref,
                       g_ref, b_ref, o_ref, key_ref, acc_ref, m_ref, *, tq, topk, alpha):
    i = pl.program_id(1)
    n_chunks = i + 1
    seq = key_ref.shape[0]
    kpos = lax.broadcasted_iota(jnp.int32, (tq, tq), 0)
    qpos = lax.broadcasted_iota(jnp.int32, (tq, tq), 1)

    def chunk_off(j):
        return pl.multiple_of(j * tq, tq)

    w_idx = kiwq_ref[...].T[HEAD_DIM:HEAD_DIM + IDX_HEADS] * (IDX_DIM ** -0.5 * IDX_HEADS ** -0.5)

    def score_chunk(j, carry):
        off = chunk_off(j)
        sc = jnp.zeros((tq, tq), F32)
        for h in range(IDX_HEADS):
            qt = qi_ref[:, (h // 2) * LANES:(h // 2 + 1) * LANES]
            s = lax.dot_general(kiab_ref[h % 2, pl.ds(off, tq), :], qt, _NT, preferred_element_type=F32)
            sc = sc + jnp.maximum(s, 0.0) * w_idx[h:h + 1]
        key = _order_key(sc)
        key = jnp.where(jnp.logical_and(j == i, kpos > qpos), INT_MIN, key)
        key_ref[pl.ds(off, tq), :] = key
        return carry

    lax.fori_loop(0, n_chunks, score_chunk, 0)

    def count_keys(pred):
        def body(j, cnt):
            off = chunk_off(j)
            ind = jnp.where(pred(key_ref[pl.ds(off, tq), :], off), 1, 0)
            return cnt + jnp.sum(ind.reshape(tq // SUBLANES, SUBLANES, tq), axis=0)
        cnt = lax.fori_loop(0, n_chunks, body, jnp.zeros((SUBLANES, tq), jnp.int32))
        return jnp.sum(cnt, axis=0, keepdims=True)

    thr = _kth_largest(lambda cand: count_keys(lambda x, off: x >= cand), (1, tq), topk)
    n_gt = count_keys(lambda x, off: x > thr)
    n_ge = count_keys(lambda x, off: x >= thr)
    tie = jnp.logical_and(n_ge > topk, thr > INT_MIN)

    @pl.when(jnp.max(jnp.where(tie, 1, 0)) > 0)
    def _():
        need = topk - n_gt
        n_bits = (seq - 1).bit_length()

        def pbody(it, p):
            cand = p + lax.shift_left(jnp.int32(1), n_bits - 1 - it)
            c = count_keys(lambda x, off: jnp.logical_and(x == thr, (kpos + off) < cand))
            return jnp.where(c < need, cand, p)
        last = lax.fori_loop(0, n_bits, pbody, jnp.zeros((1, tq), jnp.int32))

        def patch(j, carry):
            off = chunk_off(j)
            x = key_ref[pl.ds(off, tq), :]
            drop = jnp.logical_and(x == thr, (kpos + off) > last)
            key_ref[pl.ds(off, tq), :] = jnp.where(jnp.logical_and(tie, drop), thr - 1, x)
            return carry
        lax.fori_loop(0, n_chunks, patch, 0)

    thr = jnp.maximum(thr, INT_MIN + 1)

    acc_ref[...] = jnp.zeros_like(acc_ref)
    m_ref[...] = jnp.full_like(m_ref, NEG_BIG)
    lane_tiles = tq // LANES

    def attn_chunk(j, carry):
        off = chunk_off(j)
        bias = jnp.where(key_ref[pl.ds(off, tq), :] >= thr, 0.0, NEG_BIG).T
        for h in range(N_HEADS):
            kv = 2 * (h // GQA_GROUP) + h % 2
            qt = q_ref[:, (h // 2) * LANES:(h // 2 + 1) * LANES]
            s = lax.dot_general(qt, kab_ref[kv, pl.ds(off, tq), :], _NT,
                                preferred_element_type=F32) + bias
            m_prev = m_ref[h]
            m_next = jnp.maximum(m_prev, jnp.max(s, axis=1, keepdims=True))
            p = jnp.exp2(s - jnp.concatenate([m_next] * lane_tiles, axis=1))
            acc_ref[h] = (jnp.exp2(m_prev - m_next) * acc_ref[h]
                          + jnp.dot(p.astype(BF16), vab_ref[kv, pl.ds(off, tq), :],
                                    preferred_element_type=F32))
            m_ref[h] = m_next
        return carry

    lax.fori_loop(0, n_chunks, attn_chunk, 0)

    low = lax.broadcasted_iota(jnp.int32, (tq, LANES), 1) < HEAD_DIM
    tiles = []
    for t in range(N_HEADS // 2):
        a0 = acc_ref[2 * t]
        a1 = acc_ref[2 * t + 1]
        o0 = a0 / a0[:, HEAD_DIM:HEAD_DIM + 1]
        o1 = a1 / a1[:, 0:1]
        tiles.append(jnp.where(low, o0, o1).astype(BF16))
    attn = jnp.concatenate(tiles, axis=1)
    mix = jnp.dot(attn, wo_ref[...], preferred_element_type=F32)
    o_ref[...] = _layernorm(alpha * x_ref[...] + mix, g_ref[...], b_ref[...])


def _dsa_prompt(q, qi, kiw, kiab, kab, vab, x, wo, g, b, *, batch, seq, tq, alpha):
    m, d = x.shape
    assert seq % tq == 0 and tq % LANES == 0
    nq = seq // tq
    topk = min(TOPK_MAX, seq // 4)
    row = lambda bi, i: (bi * nq + i, 0)
    per_seq = lambda bi, i: (0, bi, 0)
    seq_spec = lambda n: pl.BlockSpec((n, seq, LANES), per_seq, pipeline_mode=pl.Buffered(1))
    return pl.pallas_call(
        functools.partial(_dsa_prompt_kernel, tq=tq, topk=topk, alpha=alpha),
        grid=(batch, nq),
        in_specs=[pl.BlockSpec((tq, Q_COLS), row),
                  pl.BlockSpec((tq, QI_COLS), row),
                  pl.BlockSpec((tq, LANES), row),
                  seq_spec(2), seq_spec(2 * N_KV_HEADS), seq_spec(2 * N_KV_HEADS),
                  pl.BlockSpec((tq, d), row),
                  _const_spec(wo.shape), _const_spec((1, d)), _const_spec((1, d))],
        out_specs=pl.BlockSpec((tq, d), row),
        out_shape=jax.ShapeDtypeStruct((m, d), F32),
        scratch_shapes=[pltpu.VMEM((seq, tq), jnp.int32),
                        pltpu.VMEM((N_HEADS, tq, LANES), F32),
                        pltpu.VMEM((N_HEADS, tq, LANES), F32)],
        compiler_params=_params(2),
    )(q, qi, kiw, kiab, kab, vab, x, wo, g.reshape(1, d), b.reshape(1, d))


PAGES_PER_CHUNK = 16
PAGE_KV_ROWS = PAGE_SIZE * N_KV_HEADS


def _dsa_sample_kernel(pt_ref, qi_ref, widx_ref, kinew_ref, q_ref, knew_ref, vnew_ref,
                       kidx_hbm, k_hbm, v_hbm, o_ref,
                       kib_ref, buf_ref, s_ref, key_ref, bias_ref, sem_ki, sem_kv, *, n_pages, topk):
    b = pl.program_id(0)
    n_keys = n_pages * PAGE_SIZE
    n_chunks = n_pages // PAGES_PER_CHUNK
    chunk_rows = PAGES_PER_CHUNK * PAGE_SIZE

    def page_rows(page, rows):
        return pl.ds(pl.multiple_of(pt_ref[b, page] * rows, rows), rows)

    def ki_copy(pg):
        return pltpu.make_async_copy(kidx_hbm.at[page_rows(pg, PAGE_SIZE), :],
                                     kib_ref.at[pl.ds(pl.multiple_of(pg * PAGE_SIZE, PAGE_SIZE), PAGE_SIZE), :],
                                     sem_ki)

    def chunk_copies(src_hbm, chunk, slot):
        return [pltpu.make_async_copy(src_hbm.at[page_rows(chunk * PAGES_PER_CHUNK + pg, PAGE_KV_ROWS), :],
                                      buf_ref.at[slot, pl.ds(pg * PAGE_KV_ROWS, PAGE_KV_ROWS), :],
                                      sem_kv.at[slot]) for pg in range(PAGES_PER_CHUNK)]

    def start_chunk(src_hbm, chunk, slot):
        for cp in chunk_copies(src_hbm, chunk, slot):
            cp.start()

    def wait_chunk(src_hbm, chunk, slot):
        for cp in chunk_copies(src_hbm, chunk, slot):
            cp.wait()

    def group_rows(slot, g):
        return buf_ref[slot, pl.ds(g, chunk_rows, stride=N_KV_HEADS), :].astype(BF16)

    def ki_start(pg, c):
        ki_copy(pg).start()
        return c

    def ki_wait(pg, c):
        ki_copy(pg).wait()
        return c

    lax.fori_loop(0, n_pages, ki_start, 0)
    start_chunk(k_hbm, 0, 0)
    lax.fori_loop(0, n_pages, ki_wait, 0)

    w_idx = widx_ref[0]
    qi = qi_ref[0]

    def idx_chunk(c, carry):
        rows = pl.ds(pl.multiple_of(c * chunk_rows, chunk_rows), chunk_rows)
        s_idx = lax.dot_general(qi, kib_ref[rows, :].astype(BF16), _NT, preferred_element_type=F32)
        sc = jnp.sum(jnp.maximum(s_idx, 0.0) * w_idx, axis=0, keepdims=True)
        key_ref[pl.ds(c, 1), :] = _order_key(sc)
        return carry
    lax.fori_loop(0, n_chunks, idx_chunk, 0)

    key = key_ref[...]
    s_self_idx = jnp.sum(qi.astype(F32) * kinew_ref[0], axis=1, keepdims=True)
    sc_self = jnp.sum(jnp.maximum(s_self_idx, 0.0) * w_idx, axis=0, keepdims=True)
    key_self = _order_key(sc_self)
    pos = (lax.broadcasted_iota(jnp.int32, key.shape, 0) * chunk_rows
           + lax.broadcasted_iota(jnp.int32, key.shape, 1))
    pos_self = n_keys

    def count(pred_past, pred_self):
        c = jnp.sum(jnp.sum(jnp.where(pred_past, 1, 0), axis=0, keepdims=True), axis=1, keepdims=True)
        return c + jnp.where(pred_self, 1, 0)

    thr = _kth_largest(lambda cand: count(key >= cand, key_self >= cand), (1, 1), topk)
    n_gt = count(key > thr, key_self > thr)
    n_ge = count(key >= thr, key_self >= thr)
    need = topk - n_gt
    n_bits = pos_self.bit_length()

    def pbody(it, p):
        cand = p + lax.shift_left(jnp.int32(1), n_bits - 1 - it)
        c = count(jnp.logical_and(key == thr, pos < cand),
                  jnp.logical_and(key_self == thr, pos_self < cand))
        return jnp.where(c < need, cand, p)
    last = lax.fori_loop(0, n_bits, pbody, jnp.zeros((1, 1), jnp.int32))
    last = jnp.where(n_ge > topk, last, pos_self)
    sel = jnp.logical_or(key > thr, jnp.logical_and(key == thr, pos <= last))
    sel_self = jnp.logical_or(key_self > thr, jnp.logical_and(key_self == thr, pos_self <= last))
    bias_ref[...] = jnp.where(sel, 0.0, NEG_BIG)

    head_grp = lax.broadcasted_iota(jnp.int32, (N_HEADS, 1), 0) // GQA_GROUP
    qb = q_ref[0]

    def chunk_cols(c):
        return pl.ds(pl.multiple_of(c * chunk_rows, chunk_rows), chunk_rows)

    def k_step(c, m_run):
        slot = c % 2

        @pl.when(c + 1 < n_chunks)
        def _():
            start_chunk(k_hbm, c + 1, 1 - slot)

        @pl.when(c + 1 == n_chunks)
        def _():
            start_chunk(v_hbm, 0, 1 - slot)

        wait_chunk(k_hbm, c, slot)
        s_chunk = jnp.zeros((N_HEADS, chunk_rows), F32)
        for g in range(N_KV_HEADS):
            s_g = lax.dot_general(qb, group_rows(slot, g), _NT, preferred_element_type=F32)
            s_chunk = jnp.where(head_grp == g, s_g, s_chunk)
        s_ref[:, chunk_cols(c)] = s_chunk
        return jnp.maximum(m_run, jnp.max(s_chunk + bias_ref[pl.ds(c, 1), :], axis=1, keepdims=True))

    s_self = jnp.sum(qb.astype(F32) * knew_ref[0], axis=1, keepdims=True)
    s_self = jnp.where(sel_self, s_self, NEG_BIG)
    m_head = lax.fori_loop(0, n_chunks, k_step, s_self)
    p_self = jnp.where(sel_self, jnp.exp2(s_self - m_head), 0.0)

    def v_step(c, carry):
        acc, l_head = carry
        slot = (n_chunks + c) % 2

        @pl.when(c + 1 < n_chunks)
        def _():
            start_chunk(v_hbm, c + 1, 1 - slot)

        wait_chunk(v_hbm, c, slot)
        pb = jnp.exp2(s_ref[:, chunk_cols(c)] + bias_ref[pl.ds(c, 1), :] - m_head).astype(BF16)
        l_head = l_head + jnp.sum(pb.astype(F32), axis=1, keepdims=True)
        for g in range(N_KV_HEADS):
            o_g = jnp.dot(pb, group_rows(slot, g), preferred_element_type=F32)
            acc = acc + jnp.where(head_grp == g, o_g, 0.0)
        return acc, l_head

    acc, l_head = lax.fori_loop(0, n_chunks, v_step, (jnp.zeros((N_HEADS, HEAD_DIM), F32), p_self))
    o_ref[0] = (acc + p_self * vnew_ref[0]) / l_head


def _dsa_sample(q, qi, kiw, k_new, v_new, cache_k, cache_v, cache_kidx, page_table):
    bs = q.shape[0]
    n_pages = page_table.shape[1]
    assert n_pages % PAGES_PER_CHUNK == 0
    n_chunks = n_pages // PAGES_PER_CHUNK
    chunk_rows = PAGES_PER_CHUNK * PAGE_SIZE
    n_keys = n_pages * PAGE_SIZE
    topk = min(TOPK_MAX, (n_keys + 1) // 4)

    kidx_rows = cache_kidx.reshape(-1, IDX_DIM)
    k_rows = cache_k.reshape(-1, HEAD_DIM)
    v_rows = cache_v.reshape(-1, HEAD_DIM)
    qi_h = qi.reshape(bs, IDX_HEADS, IDX_DIM)
    w_idx = (kiw[:, HEAD_DIM:HEAD_DIM + IDX_HEADS] * (IDX_DIM ** -0.5 * IDX_HEADS ** -0.5))[:, :, None]
    ki_new = jnp.broadcast_to(kiw[:, None, :IDX_DIM], (bs, IDX_HEADS, IDX_DIM))
    q_h = q.reshape(bs, N_HEADS, HEAD_DIM)
    k_rep = jnp.repeat(k_new.reshape(bs, N_KV_HEADS, HEAD_DIM), GQA_GROUP, axis=1)
    v_rep = jnp.repeat(v_new.reshape(bs, N_KV_HEADS, HEAD_DIM), GQA_GROUP, axis=1)

    per_b = lambda shape: pl.BlockSpec((1,) + shape, lambda bi, pt: (bi, 0, 0))
    any_spec = pl.BlockSpec(memory_space=pl.ANY)
    grid_spec = pltpu.PrefetchScalarGridSpec(
        num_scalar_prefetch=1,
        grid=(bs,),
        in_specs=[per_b((IDX_HEADS, IDX_DIM)), per_b((IDX_HEADS, 1)), per_b((IDX_HEADS, IDX_DIM)),
                  per_b((N_HEADS, HEAD_DIM)), per_b((N_HEADS, HEAD_DIM)), per_b((N_HEADS, HEAD_DIM)),
                  any_spec, any_spec, any_spec],
        out_specs=per_b((N_HEADS, HEAD_DIM)),
        scratch_shapes=[pltpu.VMEM((n_keys, IDX_DIM), F32),
                        pltpu.VMEM((2, PAGES_PER_CHUNK * PAGE_KV_ROWS, HEAD_DIM), F32),
                        pltpu.VMEM((N_HEADS, n_keys), F32),
                        pltpu.VMEM((n_chunks, chunk_rows), jnp.int32),
                        pltpu.VMEM((n_chunks, chunk_rows), F32),
                        pltpu.SemaphoreType.DMA(()),
                        pltpu.SemaphoreType.DMA((2,))])
    out = pl.pallas_call(
        functools.partial(_dsa_sample_kernel, n_pages=n_pages, topk=topk),
        grid_spec=grid_spec,
        out_shape=jax.ShapeDtypeStruct((bs, N_HEADS, HEAD_DIM), F32),
        compiler_params=_params(1),
    )(page_table, qi_h, w_idx, ki_new, q_h, k_rep, v_rep, kidx_rows, k_rows, v_rows)
    return out.reshape(bs, Q_COLS)


def _matmul_ln_kernel(a_ref, w_ref, x_ref, g_ref, b_ref, o_ref, *, alpha):
    mix = jnp.dot(a_ref[...].astype(BF16), w_ref[...], preferred_element_type=F32)
    o_ref[...] = _layernorm(alpha * x_ref[...] + mix, g_ref[...], b_ref[...])


def _matmul_ln(a, w, x, g, b, *, alpha):
    m, d = x.shape
    return pl.pallas_call(
        functools.partial(_matmul_ln_kernel, alpha=alpha),
        out_shape=jax.ShapeDtypeStruct((m, d), F32),
        compiler_params=pltpu.CompilerParams(vmem_limit_bytes=VMEM_LIMIT_BYTES),
    )(a, w, x, g.reshape(1, d), b.reshape(1, d))


HALO = 16


def _pool_mix(diffs, wp_ref, scale):
    outs = [jnp.dot(d.astype(BF16), wp_ref[g], preferred_element_type=F32) for g, d in enumerate(diffs)]
    return jnp.concatenate(outs, axis=1) * scale


def _pool_prompt_kernel(x_ref, halo_ref, wp_ref, sc_ref, g_ref, b_ref, o_ref, ext_ref,
                        *, tm, tiles_per_seq, alpha):
    i = pl.program_id(0)
    t_in_seq = i % tiles_per_seq
    x = x_ref[...]
    ext_ref[HALO:, :] = x
    ext_ref[:HALO, :] = jnp.where(t_in_seq == 0, 0.0, halo_ref[...])
    pos = t_in_seq * tm + lax.broadcasted_iota(jnp.int32, (tm, 1), 0)
    pool_ch = x.shape[1] // len(POOL_WINDOWS)
    diffs = []
    for g, w in enumerate(POOL_WINDOWS):
        ch = slice(g * pool_ch, (g + 1) * pool_ch)
        tot = x[:, ch]
        for s in range(1, w):
            tot = tot + ext_ref[HALO - s:HALO - s + tm, ch]
        cnt = jnp.minimum(pos + 1, w).astype(F32)
        diffs.append(tot / cnt - x[:, ch])
    mix = _pool_mix(diffs, wp_ref, sc_ref[...])
    o_ref[...] = _layernorm(alpha * x + mix, g_ref[...], b_ref[...])


def _pool_prompt(x, wp, scale, g, b, *, seq, tm, alpha):
    m, d = x.shape
    assert seq % tm == 0 and tm % HALO == 0
    tiles_per_seq = seq // tm
    halo_blocks = tm // HALO
    return pl.pallas_call(
        functools.partial(_pool_prompt_kernel, tm=tm, tiles_per_seq=tiles_per_seq, alpha=alpha),
        grid=(m // tm,),
        in_specs=[pl.BlockSpec((tm, d), lambda i: (i, 0)),
                  pl.BlockSpec((HALO, d), lambda i: (jnp.maximum(i * halo_blocks - 1, 0), 0)),
                  _const_spec(wp.shape), _const_spec((1, d)), _const_spec((1, d)), _const_spec((1, d))],
        out_specs=pl.BlockSpec((tm, d), lambda i: (i, 0)),
        out_shape=jax.ShapeDtypeStruct((m, d), F32),
        scratch_shapes=[pltpu.VMEM((tm + HALO, d), F32)],
        compiler_params=_params(1),
    )(x, x, wp, scale.reshape(1, d), g.reshape(1, d), b.reshape(1, d))


def _pool_sample_kernel(ext_ref, wp_ref, sc_ref, g_ref, b_ref, o_ref, *, alpha):
    n_rows = ext_ref.shape[1]
    x = ext_ref[:, n_rows - 1, :]
    pool_ch = x.shape[1] // len(POOL_WINDOWS)
    diffs = []
    for g, w in enumerate(POOL_WINDOWS):
        ch = slice(g * pool_ch, (g + 1) * pool_ch)
        tot = x[:, ch]
        for s in range(1, w):
            tot = tot + ext_ref[:, n_rows - 1 - s, ch]
        diffs.append(tot / float(w) - x[:, ch])
    mix = _pool_mix(diffs, wp_ref, sc_ref[...])
    o_ref[...] = _layernorm(alpha * x + mix, g_ref[...], b_ref[...])


def _pool_sample(ext, wp, scale, g, b, *, alpha):
    bs, n_rows, d = ext.shape
    assert n_rows >= max(POOL_WINDOWS)
    return pl.pallas_call(
        functools.partial(_pool_sample_kernel, alpha=alpha),
        out_shape=jax.ShapeDtypeStruct((bs, d), F32),
        compiler_params=pltpu.CompilerParams(vmem_limit_bytes=VMEM_LIMIT_BYTES),
    )(ext, wp, scale.reshape(1, d), g.reshape(1, d), b.reshape(1, d))


def _row_tile(m, target):
    return target if m % target == 0 else m


def kernel(x_prompt, x_sample, cache_k, cache_v, cache_kidx, state_pool, page_table, ln_g, ln_b,
           ffn1_wi, ffn1_wo, ffn2_wi, ffn2_wo, attn_w_in, attn_w_o, pool_w, pool_scale):
    batch, seq, d = x_prompt.shape
    bs, dec_seq, _ = x_sample.shape
    assert dec_seq == 1
    depth = ln_g.shape[0]
    alpha = (2 * depth) ** 0.25
    past = page_table.shape[1] * PAGE_SIZE

    xp = x_prompt.reshape(batch * seq, d)
    xs = x_sample.reshape(bs, d)
    tm_p = _row_tile(batch * seq, 512)
    tq = _row_tile(seq, 256)

    def ffn(x, wi, wo, g, b, tm):
        return _ffn_ln(x, wi.astype(BF16), wo.astype(BF16), g, b, tm=tm, alpha=alpha)

    tabs_p = _rope_tables(jnp.arange(seq, dtype=jnp.int32).astype(F32))
    tabs_s = _rope_tables((past + jnp.zeros((bs,), jnp.int32)).astype(F32))

    kp_l, vp_l, kip_l, poolp_l = [], [], [], []
    ks_l, vs_l, kis_l, pools_l = [], [], [], []
    for l in range(depth):
        j = l // N_MIXERS
        xp = ffn(xp, ffn1_wi[l], ffn1_wo[l], ln_g[l, 0], ln_b[l, 0], tm_p)
        xs = ffn(xs, ffn1_wi[l], ffn1_wo[l], ln_g[l, 0], ln_b[l, 0], bs)
        if l % N_MIXERS == 0:
            w_in = attn_w_in[j].astype(BF16)
            w_in = jnp.pad(w_in, ((0, 0), (0, W_IN_PAD - w_in.shape[1])))
            w_o = attn_w_o[j].astype(BF16)
            q, k, v, qi, kiw, kab, vab, kiab = _attn_proj(xp, w_in, tabs_p, tm=tm_p, attn_layouts=True)
            xp = _dsa_prompt(q, qi, kiw, kiab, kab, vab, xp, w_o, ln_g[l, 1], ln_b[l, 1],
                             batch=batch, seq=seq, tq=tq, alpha=alpha)
            kp_l.append(k.reshape(batch, seq, N_KV_HEADS, HEAD_DIM))
            vp_l.append(v.reshape(batch, seq, N_KV_HEADS, HEAD_DIM))
            kip_l.append(kiw[:, :IDX_DIM].reshape(batch, seq, IDX_DIM))

            qs, k_s, v_s, qis, kiws = _attn_proj(xs, w_in, tabs_s, tm=bs, attn_layouts=False)
            attn_s = _dsa_sample(qs, qis, kiws, k_s, v_s, cache_k[j], cache_v[j], cache_kidx[j],
                                 page_table)
            xs = _matmul_ln(attn_s, w_o, xs, ln_g[l, 1], ln_b[l, 1], alpha=alpha)
            ks_l.append(k_s.reshape(bs, 1, N_KV_HEADS, HEAD_DIM))
            vs_l.append(v_s.reshape(bs, 1, N_KV_HEADS, HEAD_DIM))
            kis_l.append(kiws[:, :IDX_DIM].reshape(bs, 1, IDX_DIM))
        else:
            wp = pool_w[j].astype(BF16)
            poolp_l.append(xp.reshape(batch, seq, d)[:, seq - POOL_BUF:])
            ext = jnp.concatenate([state_pool[j], xs[:, None, :]], axis=1)
            pools_l.append(ext[:, ext.shape[1] - POOL_BUF:])
            xp = _pool_prompt(xp, wp, pool_scale[j], ln_g[l, 1], ln_b[l, 1],
                              seq=seq, tm=_row_tile(seq, 512), alpha=alpha)
            xs = _pool_sample(ext, wp, pool_scale[j], ln_g[l, 1], ln_b[l, 1], alpha=alpha)
        xp = ffn(xp, ffn2_wi[l], ffn2_wo[l], ln_g[l, 2], ln_b[l, 2], tm_p)
        xs = ffn(xs, ffn2_wi[l], ffn2_wo[l], ln_g[l, 2], ln_b[l, 2], bs)

    return (xp.reshape(batch, seq, d), xs.reshape(bs, 1, d),
            jnp.stack(kp_l), jnp.stack(vp_l), jnp.stack(kip_l), jnp.stack(poolp_l),
            jnp.stack(ks_l), jnp.stack(vs_l), jnp.stack(kis_l), jnp.stack(pools_l))
```

```python
import functools

import jax
import jax.numpy as jnp
from jax import lax
from jax.experimental import pallas as pl
from jax.experimental.pallas import tpu as pltpu

N_MIXERS = 2
N_HEADS = 16
HEAD_DIM = 64
N_KV_HEADS = 4
GQA_GROUP = N_HEADS // N_KV_HEADS
ROT_DIM = HEAD_DIM // 4
ROPE_THETA = 500000.0
IDX_HEADS = 8
IDX_DIM = 64
TOPK_MAX = 256
PAGE_SIZE = 128
POOL_WINDOWS = (2, 4, 8, 16)
POOL_BUF = max(POOL_WINDOWS) - 1
LN_EPS = 1e-5
Q_COLS = N_HEADS * HEAD_DIM
KV_COLS = N_KV_HEADS * HEAD_DIM
QI_COLS = IDX_HEADS * IDX_DIM

LANES = 128
SUBLANES = 8
BF16_ROWS = 16
VMEM_LIMIT_BYTES = 56 * 1024 * 1024

F32 = jnp.float32
BF16 = jnp.bfloat16
INT_MIN = -(2 ** 31)
NEG_BIG = -1e30

_NT = (((1,), (1,)), ((), ()))


def _layernorm(y, g, b):
    mu = jnp.mean(y, axis=-1, keepdims=True)
    yc = y - mu
    var = jnp.mean(yc * yc, axis=-1, keepdims=True)
    return yc * lax.rsqrt(var + LN_EPS) * g + b


def _const_spec(shape):
    nd = len(shape)
    return pl.BlockSpec(shape, lambda *_: (0,) * nd, pipeline_mode=pl.Buffered(1))


def _params(n_grid):
    return pltpu.CompilerParams(dimension_semantics=("arbitrary",) * n_grid,
                                vmem_limit_bytes=VMEM_LIMIT_BYTES)


FFN_CHUNK = 256


def _ffn_kernel(x_ref, wi_ref, wo_ref, g_ref, b_ref, o_ref, acc_ref, *, d_ff, alpha):
    x = x_ref[...]
    xb = x.astype(BF16)
    for c in range(d_ff // FFN_CHUNK):
        lo = c * FFN_CHUNK
        hg = jnp.dot(xb, wi_ref[:, lo:lo + FFN_CHUNK], preferred_element_type=F32)
        hu = jnp.dot(xb, wi_ref[:, d_ff + lo:d_ff + lo + FFN_CHUNK], preferred_element_type=F32)
        a = (hg * jax.nn.sigmoid(hg) * hu).astype(BF16)
        part = jnp.dot(a, wo_ref[lo:lo + FFN_CHUNK, :], preferred_element_type=F32)
        if c == 0:
            acc_ref[...] = part
        else:
            acc_ref[...] += part
    o_ref[...] = _layernorm(alpha * x + 0.5 * acc_ref[...], g_ref[...], b_ref[...])


def _ffn_ln(x, wi, wo, g, b, *, tm, alpha):
    m, d = x.shape
    d_ff = wo.shape[0]
    assert m % tm == 0 and d_ff % FFN_CHUNK == 0
    return pl.pallas_call(
        functools.partial(_ffn_kernel, d_ff=d_ff, alpha=alpha),
        grid=(m // tm,),
        in_specs=[pl.BlockSpec((tm, d), lambda i: (i, 0)),
                  _const_spec(wi.shape), _const_spec(wo.shape),
                  _const_spec((1, d)), _const_spec((1, d))],
        out_specs=pl.BlockSpec((tm, d), lambda i: (i, 0)),
        out_shape=jax.ShapeDtypeStruct((m, d), F32),
        scratch_shapes=[pltpu.VMEM((tm, d), F32)],
        compiler_params=_params(1),
    )(x, wi, wo, g.reshape(1, d), b.reshape(1, d))


O_K = Q_COLS
O_V = O_K + KV_COLS
O_QI = O_V + KV_COLS
O_KI = O_QI + QI_COLS
W_IN_PAD = O_KI + LANES
LOG2E = 1.4426950408889634
Q_SCALE = HEAD_DIM ** -0.5 * LOG2E


def _proj_kernel(x_ref, w_ref, cos_ref, sa_ref, sb_ref, q_ref, k_ref, v_ref, qi_ref, kiw_ref,
                 *attn_refs):
    xb = x_ref[...].astype(BF16)
    cos = cos_ref[...]
    sin_a = sa_ref[...]
    sin_b = sb_ref[...]
    tm = xb.shape[0]
    low = lax.broadcasted_iota(jnp.int32, (tm, LANES), 1) < HEAD_DIM

    def rope(t):
        return (t * cos + pltpu.roll(t, ROT_DIM // 2, 1) * sin_a
                + pltpu.roll(t, LANES - ROT_DIM // 2, 1) * sin_b)

    def proj(lo, width):
        return jnp.dot(xb, w_ref[:, lo:lo + width], preferred_element_type=F32)

    pq = proj(0, Q_COLS)
    for t in range(Q_COLS // LANES):
        q_ref[:, t * LANES:(t + 1) * LANES] = (
            rope(pq[:, t * LANES:(t + 1) * LANES]) * Q_SCALE).astype(BF16)
    pqi = proj(O_QI, QI_COLS)
    for t in range(QI_COLS // LANES):
        qi_ref[:, t * LANES:(t + 1) * LANES] = rope(pqi[:, t * LANES:(t + 1) * LANES]).astype(BF16)
    pkw = proj(O_KI, LANES)
    kw = jnp.where(low, rope(pkw), pkw)
    kiw_ref[...] = kw

    pk = proj(O_K, KV_COLS)
    pv = proj(O_V, KV_COLS)
    if attn_refs:
        kab_ref, vab_ref, kiab_ref = attn_refs
        kiab_ref[0] = jnp.where(low, kw, 0.0).astype(BF16)
        kiab_ref[1] = jnp.where(low, 0.0, pltpu.roll(kw, HEAD_DIM, 1)).astype(BF16)
        lane = lax.broadcasted_iota(jnp.int32, (tm, LANES), 1)
        ones_a = jnp.where(lane == HEAD_DIM, 1.0, 0.0)
        ones_b = jnp.where(lane == 0, 1.0, 0.0)
    for t in range(KV_COLS // LANES):
        kr = rope(pk[:, t * LANES:(t + 1) * LANES])
        k_ref[:, t * LANES:(t + 1) * LANES] = kr
        vv = pv[:, t * LANES:(t + 1) * LANES]
        v_ref[:, t * LANES:(t + 1) * LANES] = vv
        if attn_refs:
            krr = pltpu.roll(kr, HEAD_DIM, 1)
            g0, g1 = 2 * t, 2 * t + 1
            kab_ref[2 * g0] = jnp.where(low, kr, 0.0).astype(BF16)
            kab_ref[2 * g0 + 1] = jnp.where(low, 0.0, krr).astype(BF16)
            kab_ref[2 * g1] = jnp.where(low, krr, 0.0).astype(BF16)
            kab_ref[2 * g1 + 1] = jnp.where(low, 0.0, kr).astype(BF16)
            vvr = pltpu.roll(vv, HEAD_DIM, 1)
            vab_ref[2 * g0] = jnp.where(low, vv, ones_a).astype(BF16)
            vab_ref[2 * g0 + 1] = jnp.where(low, ones_b, vvr).astype(BF16)
            vab_ref[2 * g1] = jnp.where(low, vvr, ones_a).astype(BF16)
            vab_ref[2 * g1 + 1] = jnp.where(low, ones_b, vv).astype(BF16)


def _rope_tables(pos):
    half = ROT_DIM // 2
    freqs = ROPE_THETA ** (-jnp.arange(half, dtype=F32) / half)
    ang = pos[:, None] * freqs[None, :]
    cos, sin = jnp.cos(ang), jnp.sin(ang)
    t = pos.shape[0]
    rest = HEAD_DIM - ROT_DIM
    zeros_h = jnp.zeros((t, half), F32)
    c64 = jnp.concatenate([cos, cos, jnp.ones((t, rest), F32)], axis=1)
    a64 = jnp.concatenate([zeros_h, sin, jnp.zeros((t, rest), F32)], axis=1)
    b64 = jnp.concatenate([-sin, zeros_h, jnp.zeros((t, rest), F32)], axis=1)
    rep = LANES // HEAD_DIM
    return tuple(jnp.tile(a, (1, rep)) for a in (c64, a64, b64))


def _attn_proj(x, w_pad, tables, *, tm, attn_layouts):
    m, d = x.shape
    t_rows = tables[0].shape[0]
    assert m % tm == 0 and t_rows % tm == 0
    n_t = t_rows // tm
    row = lambda i: (i, 0)
    tab_spec = pl.BlockSpec((tm, LANES), lambda i: (i % n_t, 0))
    n_kv2 = 2 * N_KV_HEADS
    out_specs = [pl.BlockSpec((tm, Q_COLS), row), pl.BlockSpec((tm, KV_COLS), row),
                 pl.BlockSpec((tm, KV_COLS), row), pl.BlockSpec((tm, QI_COLS), row),
                 pl.BlockSpec((tm, LANES), row)]
    out_shape = [jax.ShapeDtypeStruct((m, Q_COLS), BF16), jax.ShapeDtypeStruct((m, KV_COLS), F32),
                 jax.ShapeDtypeStruct((m, KV_COLS), F32), jax.ShapeDtypeStruct((m, QI_COLS), BF16),
                 jax.ShapeDtypeStruct((m, LANES), F32)]
    if attn_layouts:
        out_specs += [pl.BlockSpec((n_kv2, tm, LANES), lambda i: (0, i, 0)),
                      pl.BlockSpec((n_kv2, tm, LANES), lambda i: (0, i, 0)),
                      pl.BlockSpec((2, tm, LANES), lambda i: (0, i, 0))]
        out_shape += [jax.ShapeDtypeStruct((n_kv2, m, LANES), BF16),
                      jax.ShapeDtypeStruct((n_kv2, m, LANES), BF16),
                      jax.ShapeDtypeStruct((2, m, LANES), BF16)]
    return pl.pallas_call(
        _proj_kernel,
        grid=(m // tm,),
        in_specs=[pl.BlockSpec((tm, d), row), _const_spec(w_pad.shape), tab_spec, tab_spec, tab_spec],
        out_specs=out_specs,
        out_shape=out_shape,
        compiler_params=_params(1),
    )(x, w_pad, *tables)


def _order_key(score):
    score = jnp.where(score == 0.0, 0.0, score)
    bits = pltpu.bitcast(score, jnp.int32)
    return bits ^ ((bits >> 31) & 0x7FFFFFFF)


def _kth_largest(count_ge, shape, k):
    def body(it, t):
        cand = t + lax.shift_left(jnp.int32(1), 31 - it)
        return jnp.where(count_ge(cand) >= k, cand, t)
    return lax.fori_loop(0, 32, body, jnp.full(shape, INT_MIN, jnp.int32))


def _dsa_prompt_kernel(q_ref, qi_ref, kiwq_ref, kiab_ref, kab_ref, vab_ref, x_ref, wo_ref,
                       g_ref, b_ref, o_ref, key_ref, acc_ref, m_ref, *, tq, topk, alpha):
    i = pl.program_id(1)
    n_chunks = i + 1
    seq = key_ref.shape[0]
    kpos = lax.broadcasted_iota(jnp.int32, (tq, tq), 0)
    qpos = lax.broadcasted_iota(jnp.int32, (tq, tq), 1)

    def chunk_off(j):
        return pl.multiple_of(j * tq, tq)

    w_idx = kiwq_ref[...].T[HEAD_DIM:HEAD_DIM + IDX_HEADS] * (IDX_DIM ** -0.5 * IDX_HEADS ** -0.5)

    def score_chunk(j, carry):
        off = chunk_off(j)
        sc = jnp.zeros((tq, tq), F32)
        for h in range(IDX_HEADS):
            qt = qi_ref[:, (h // 2) * LANES:(h // 2 + 1) * LANES]
            s = lax.dot_general(kiab_ref[h % 2, pl.ds(off, tq), :], qt, _NT, preferred_element_type=F32)
            sc = sc + jnp.maximum(s, 0.0) * w_idx[h:h + 1]
        key = _order_key(sc)
        key = jnp.where(jnp.logical_and(j == i, kpos > qpos), INT_MIN, key)
        key_ref[pl.ds(off, tq), :] = key
        return carry

    lax.fori_loop(0, n_chunks, score_chunk, 0)

    def count_keys(pred):
        def body(j, cnt):
            off = chunk_off(j)
            ind = jnp.where(pred(key_ref[pl.ds(off, tq), :], off), 1, 0)
            return cnt + jnp.sum(ind.reshape(tq // SUBLANES, SUBLANES, tq), axis=0)
        cnt = lax.fori_loop(0, n_chunks, body, jnp.zeros((SUBLANES, tq), jnp.int32))
        return jnp.sum(cnt, axis=0, keepdims=True)

    thr = _kth_largest(lambda cand: count_keys(lambda x, off: x >= cand), (1, tq), topk)
    n_gt = count_keys(lambda x, off: x > thr)
    n_ge = count_keys(lambda x, off: x >= thr)
    tie = jnp.logical_and(n_ge > topk, thr > INT_MIN)

    @pl.when(jnp.max(jnp.where(tie, 1, 0)) > 0)
    def _():
        need = topk - n_gt
        n_bits = (seq - 1).bit_length()

        def pbody(it, p):
            cand = p + lax.shift_left(jnp.int32(1), n_bits - 1 - it)
            c = count_keys(lambda x, off: jnp.logical_and(x == thr, (kpos + off) < cand))
            return jnp.where(c < need, cand, p)
        last = lax.fori_loop(0, n_bits, pbody, jnp.zeros((1, tq), jnp.int32))

        def patch(j, carry):
            off = chunk_off(j)
            x = key_ref[pl.ds(off, tq), :]
            drop = jnp.logical_and(x == thr, (kpos + off) > last)
            key_ref[pl.ds(off, tq), :] = jnp.where(jnp.logical_and(tie, drop), thr - 1, x)
            return carry
        lax.fori_loop(0, n_chunks, patch, 0)

    thr = jnp.maximum(thr, INT_MIN + 1)

    acc_ref[...] = jnp.zeros_like(acc_ref)
    m_ref[...] = jnp.full_like(m_ref, NEG_BIG)
    lane_tiles = tq // LANES

    def attn_chunk(j, carry):
        off = chunk_off(j)
        bias = jnp.where(key_ref[pl.ds(off, tq), :] >= thr, 0.0, NEG_BIG).T
        for h in range(N_HEADS):
            kv = 2 * (h // GQA_GROUP) + h % 2
            qt = q_ref[:, (h // 2) * LANES:(h // 2 + 1) * LANES]
            s = lax.dot_general(qt, kab_ref[kv, pl.ds(off, tq), :], _NT,
                                preferred_element_type=F32) + bias
            m_prev = m_ref[h]
            m_next = jnp.maximum(m_prev, jnp.max(s, axis=1, keepdims=True))
            p = jnp.exp2(s - jnp.concatenate([m_next] * lane_tiles, axis=1))
            acc_ref[h] = (jnp.exp2(m_prev - m_next) * acc_ref[h]
                          + jnp.dot(p.astype(BF16), vab_ref[kv, pl.ds(off, tq), :],
                                    preferred_element_type=F32))
            m_ref[h] = m_next
        return carry

    lax.fori_loop(0, n_chunks, attn_chunk, 0)

    low = lax.broadcasted_iota(jnp.int32, (tq, LANES), 1) < HEAD_DIM
    tiles = []
    for t in range(N_HEADS // 2):
        a0 = acc_ref[2 * t]
        a1 = acc_ref[2 * t + 1]
        o0 = a0 / a0[:, HEAD_DIM:HEAD_DIM + 1]
        o1 = a1 / a1[:, 0:1]
        tiles.append(jnp.where(low, o0, o1).astype(BF16))
    attn = jnp.concatenate(tiles, axis=1)
    mix = jnp.dot(attn, wo_ref[...], preferred_element_type=F32)
    o_ref[...] = _layernorm(alpha * x_ref[...] + mix, g_ref[...], b_ref[...])


def _dsa_prompt(q, qi, kiw, kiab, kab, vab, x, wo, g, b, *, batch, seq, tq, alpha):
    m, d = x.shape
    assert seq % tq == 0 and tq % LANES == 0
    nq = seq // tq
    topk = min(TOPK_MAX, seq // 4)
    row = lambda bi, i: (bi * nq + i, 0)
    per_seq = lambda bi, i: (0, bi, 0)
    seq_spec = lambda n: pl.BlockSpec((n, seq, LANES), per_seq, pipeline_mode=pl.Buffered(1))
    return pl.pallas_call(
        functools.partial(_dsa_prompt_kernel, tq=tq, topk=topk, alpha=alpha),
        grid=(batch, nq),
        in_specs=[pl.BlockSpec((tq, Q_COLS), row),
                  pl.BlockSpec((tq, QI_COLS), row),
                  pl.BlockSpec((tq, LANES), row),
                  seq_spec(2), seq_spec(2 * N_KV_HEADS), seq_spec(2 * N_KV_HEADS),
                  pl.BlockSpec((tq, d), row),
                  _const_spec(wo.shape), _const_spec((1, d)), _const_spec((1, d))],
        out_specs=pl.BlockSpec((tq, d), row),
        out_shape=jax.ShapeDtypeStruct((m, d), F32),
        scratch_shapes=[pltpu.VMEM((seq, tq), jnp.int32),
                        pltpu.VMEM((N_HEADS, tq, LANES), F32),
                        pltpu.VMEM((N_HEADS, tq, LANES), F32)],
        compiler_params=_params(2),
    )(q, qi, kiw, kiab, kab, vab, x, wo, g.reshape(1, d), b.reshape(1, d))


PAGES_PER_CHUNK = 16
CHUNK_TOKENS = PAGES_PER_CHUNK * PAGE_SIZE


def _chunk_cols(c):
    return pl.ds(pl.multiple_of(c * CHUNK_TOKENS, CHUNK_TOKENS), CHUNK_TOKENS)


def _page_cols(pg):
    return pl.ds(pl.multiple_of(pg * PAGE_SIZE, PAGE_SIZE), PAGE_SIZE)


def _idx_sample_kernel(pt_ref, qi_ref, widx_ref, kinew_ref, kidx_hbm, key_ref, keyself_ref,
                       kib_ref, sem, *, n_pages):
    b = pl.program_id(0)
    n_b = pl.num_programs(0)
    slot = b % 2
    n_chunks = n_pages // PAGES_PER_CHUNK

    def page_copy(sample, sl, pg):
        return pltpu.make_async_copy(kidx_hbm.at[pt_ref[sample, pg]],
                                     kib_ref.at[sl, :, _page_cols(pg)], sem.at[sl])

    def start_sample(sample, sl):
        def body(pg, c):
            page_copy(sample, sl, pg).start()
            return c
        lax.fori_loop(0, n_pages, body, 0)

    def wait_sample(sample, sl):
        def body(pg, c):
            page_copy(sample, sl, pg).wait()
            return c
        lax.fori_loop(0, n_pages, body, 0)

    @pl.when(b == 0)
    def _():
        start_sample(0, 0)

    @pl.when(b + 1 < n_b)
    def _():
        start_sample(b + 1, 1 - slot)

    wait_sample(b, slot)
    w_idx = widx_ref[0]
    qi = qi_ref[0]

    def idx_chunk(c, carry):
        s_idx = jnp.dot(qi, kib_ref[slot, :, _chunk_cols(c)].astype(BF16), preferred_element_type=F32)
        sc = jnp.sum(jnp.maximum(s_idx, 0.0) * w_idx, axis=0, keepdims=True)
        key_ref[0, pl.ds(c, 1), :] = _order_key(sc)
        return carry
    lax.fori_loop(0, n_chunks, idx_chunk, 0)

    s_self = jnp.sum(qi.astype(F32) * kinew_ref[0], axis=1, keepdims=True)
    sc_self = jnp.sum(jnp.maximum(s_self, 0.0) * w_idx, axis=0, keepdims=True)
    keyself_ref[0] = jnp.broadcast_to(_order_key(sc_self), (1, LANES))


def _select_sample_kernel(key_ref, keyself_ref, bias_ref, selself_ref, *, topk):
    bs, n_keys = key_ref.shape
    key_self = keyself_ref[:, 0:1]
    pos = lax.broadcasted_iota(jnp.int32, (bs, n_keys), 1)
    pos_self = n_keys

    def count(pred_past, pred_self):
        c = jnp.sum(jnp.where(pred_past(key_ref[...]), 1, 0), axis=1, keepdims=True)
        return c + jnp.where(pred_self, 1, 0)

    thr = _kth_largest(lambda cand: count(lambda k: k >= cand, key_self >= cand), (bs, 1), topk)
    n_gt = count(lambda k: k > thr, key_self > thr)
    n_ge = count(lambda k: k >= thr, key_self >= thr)
    need = topk - n_gt
    n_bits = pos_self.bit_length()

    def pbody(it, p):
        cand = p + lax.shift_left(jnp.int32(1), n_bits - 1 - it)
        c = count(lambda k: jnp.logical_and(k == thr, pos < cand),
                  jnp.logical_and(key_self == thr, pos_self < cand))
        return jnp.where(c < need, cand, p)
    last = lax.fori_loop(0, n_bits, pbody, jnp.zeros((bs, 1), jnp.int32))
    last = jnp.where(n_ge > topk, last, pos_self)
    key = key_ref[...]
    sel = jnp.logical_or(key > thr, jnp.logical_and(key == thr, pos <= last))
    sel_self = jnp.logical_or(key_self > thr, jnp.logical_and(key_self == thr, pos_self <= last))
    bias_ref[...] = jnp.where(sel, 0.0, NEG_BIG)
    selself_ref[...] = jnp.broadcast_to(jnp.where(sel_self, 1.0, 0.0), selself_ref.shape)


def _attn_sample_kernel(pt_ref, q_ref, knew_ref, vnew_ref, bias_ref, selself_ref, k_hbm, v_hbm, o_ref,
                        buf_ref, sem, *, n_pages):
    b = pl.program_id(0)
    n_b = pl.num_programs(0)
    n_chunks = n_pages // PAGES_PER_CHUNK

    def chunk_copies(sample, c, sl):
        cps = []
        for pg in range(PAGES_PER_CHUNK):
            page = pt_ref[sample, c * PAGES_PER_CHUNK + pg]
            cols = pl.ds(pg * PAGE_SIZE, PAGE_SIZE)
            for which, src in enumerate((k_hbm, v_hbm)):
                cps.append(pltpu.make_async_copy(src.at[page], buf_ref.at[sl, which, :, :, cols], sem.at[sl]))
        return cps

    def start_chunk(sample, c, sl):
        for cp in chunk_copies(sample, c, sl):
            cp.start()

    def wait_chunk(sample, c, sl):
        for cp in chunk_copies(sample, c, sl):
            cp.wait()

    @pl.when(b == 0)
    def _():
        start_chunk(0, 0, 0)

    head_grp = lax.broadcasted_iota(jnp.int32, (N_HEADS, 1), 0) // GQA_GROUP
    qb = q_ref[0]
    sel_self = selself_ref[0, :, 0:1] > 0.0

    def step(c, carry):
        m_run, l_run, acc = carry
        sl = c % 2

        @pl.when(c + 1 < n_chunks)
        def _():
            start_chunk(b, c + 1, 1 - sl)

        @pl.when(jnp.logical_and(c + 1 == n_chunks, b + 1 < n_b))
        def _():
            start_chunk(b + 1, 0, 1 - sl)

        wait_chunk(b, c, sl)
        s = jnp.zeros((N_HEADS, CHUNK_TOKENS), F32)
        for g in range(N_KV_HEADS):
            s_g = jnp.dot(qb, buf_ref[sl, 0, g].astype(BF16), preferred_element_type=F32)
            s = jnp.where(head_grp == g, s_g, s)
        s = s + bias_ref[0, :, _chunk_cols(c)]
        m_new = jnp.maximum(m_run, jnp.max(s, axis=1, keepdims=True))
        p = jnp.exp2(s - m_new).astype(BF16)
        scale = jnp.exp2(m_run - m_new)
        l_new = scale * l_run + jnp.sum(p.astype(F32), axis=1, keepdims=True)
        pv = jnp.zeros((N_HEADS, HEAD_DIM), F32)
        for g in range(N_KV_HEADS):
            o_g = lax.dot_general(p, buf_ref[sl, 1, g].astype(BF16), _NT, preferred_element_type=F32)
            pv = jnp.where(head_grp == g, o_g, pv)
        return m_new, l_new, scale * acc + pv

    s_self = jnp.sum(qb.astype(F32) * knew_ref[0], axis=1, keepdims=True)
    m0 = jnp.where(sel_self, s_self, NEG_BIG)
    l0 = jnp.where(sel_self, 1.0, 0.0) + jnp.zeros_like(s_self)
    _, l_fin, acc = lax.fori_loop(0, n_chunks, step, (m0, l0, l0 * vnew_ref[0]))
    o_ref[0] = acc / l_fin


def _dsa_sample(q, qi, kiw, k_new, v_new, cache_k, cache_v, cache_kidx, page_table):
    bs = q.shape[0]
    n_pages = page_table.shape[1]
    assert n_pages % (2 * PAGES_PER_CHUNK) == 0
    n_chunks = n_pages // PAGES_PER_CHUNK
    n_keys = n_pages * PAGE_SIZE
    topk = min(TOPK_MAX, (n_keys + 1) // 4)

    kidx_t = jnp.transpose(cache_kidx, (0, 2, 1))
    k_t = jnp.transpose(cache_k, (0, 2, 3, 1))
    v_t = jnp.transpose(cache_v, (0, 2, 3, 1))
    qi_h = qi.reshape(bs, IDX_HEADS, IDX_DIM)
    w_idx = (kiw[:, HEAD_DIM:HEAD_DIM + IDX_HEADS] * (IDX_DIM ** -0.5 * IDX_HEADS ** -0.5))[:, :, None]
    ki_new = jnp.broadcast_to(kiw[:, None, :IDX_DIM], (bs, IDX_HEADS, IDX_DIM))
    q_h = q.reshape(bs, N_HEADS, HEAD_DIM)
    k_rep = jnp.repeat(k_new.reshape(bs, N_KV_HEADS, HEAD_DIM), GQA_GROUP, axis=1)
    v_rep = jnp.repeat(v_new.reshape(bs, N_KV_HEADS, HEAD_DIM), GQA_GROUP, axis=1)

    per_b = lambda shape: pl.BlockSpec((1,) + shape, lambda bi, pt: (bi, 0, 0))
    any_spec = pl.BlockSpec(memory_space=pl.ANY)
    keys, key_self = pl.pallas_call(
        functools.partial(_idx_sample_kernel, n_pages=n_pages),
        grid_spec=pltpu.PrefetchScalarGridSpec(
            num_scalar_prefetch=1, grid=(bs,),
            in_specs=[per_b((IDX_HEADS, IDX_DIM)), per_b((IDX_HEADS, 1)), per_b((IDX_HEADS, IDX_DIM)),
                      any_spec],
            out_specs=[per_b((n_chunks, CHUNK_TOKENS)), per_b((1, LANES))],
            scratch_shapes=[pltpu.VMEM((2, IDX_DIM, n_keys), F32), pltpu.SemaphoreType.DMA((2,))]),
        out_shape=[jax.ShapeDtypeStruct((bs, n_chunks, CHUNK_TOKENS), jnp.int32),
                   jax.ShapeDtypeStruct((bs, 1, LANES), jnp.int32)],
        compiler_params=_params(1),
    )(page_table, qi_h, w_idx, ki_new, kidx_t)

    bias, sel_self = pl.pallas_call(
        functools.partial(_select_sample_kernel, topk=topk),
        out_shape=[jax.ShapeDtypeStruct((bs, n_keys), F32), jax.ShapeDtypeStruct((bs, LANES), F32)],
        compiler_params=pltpu.CompilerParams(vmem_limit_bytes=VMEM_LIMIT_BYTES),
    )(keys.reshape(bs, n_keys), key_self.reshape(bs, LANES))

    out = pl.pallas_call(
        functools.partial(_attn_sample_kernel, n_pages=n_pages),
        grid_spec=pltpu.PrefetchScalarGridSpec(
            num_scalar_prefetch=1, grid=(bs,),
            in_specs=[per_b((N_HEADS, HEAD_DIM)), per_b((N_HEADS, HEAD_DIM)), per_b((N_HEADS, HEAD_DIM)),
                      per_b((1, n_keys)), per_b((1, LANES)), any_spec, any_spec],
            out_specs=per_b((N_HEADS, HEAD_DIM)),
            scratch_shapes=[pltpu.VMEM((2, 2, N_KV_HEADS, HEAD_DIM, CHUNK_TOKENS), F32),
                            pltpu.SemaphoreType.DMA((2,))]),
        out_shape=jax.ShapeDtypeStruct((bs, N_HEADS, HEAD_DIM), F32),
        compiler_params=_params(1),
    )(page_table, q_h, k_rep, v_rep, bias.reshape(bs, 1, n_keys), sel_self.reshape(bs, 1, LANES), k_t, v_t)
    return out.reshape(bs, Q_COLS)


def _matmul_ln_kernel(a_ref, w_ref, x_ref, g_ref, b_ref, o_ref, *, alpha):
    mix = jnp.dot(a_ref[...].astype(BF16), w_ref[...], preferred_element_type=F32)
    o_ref[...] = _layernorm(alpha * x_ref[...] + mix, g_ref[...], b_ref[...])


def _matmul_ln(a, w, x, g, b, *, alpha):
    m, d = x.shape
    return pl.pallas_call(
        functools.partial(_matmul_ln_kernel, alpha=alpha),
        out_shape=jax.ShapeDtypeStruct((m, d), F32),
        compiler_params=pltpu.CompilerParams(vmem_limit_bytes=VMEM_LIMIT_BYTES),
    )(a, w, x, g.reshape(1, d), b.reshape(1, d))


HALO = 16


def _pool_mix(diffs, wp_ref, scale):
    outs = [jnp.dot(d.astype(BF16), wp_ref[g], preferred_element_type=F32) for g, d in enumerate(diffs)]
    return jnp.concatenate(outs, axis=1) * scale


def _pool_prompt_kernel(x_ref, halo_ref, wp_ref, sc_ref, g_ref, b_ref, o_ref, ext_ref,
                        *, tm, tiles_per_seq, alpha):
    i = pl.program_id(0)
    t_in_seq = i % tiles_per_seq
    x = x_ref[...]
    ext_ref[HALO:, :] = x
    ext_ref[:HALO, :] = jnp.where(t_in_seq == 0, 0.0, halo_ref[...])
    pos = t_in_seq * tm + lax.broadcasted_iota(jnp.int32, (tm, 1), 0)
    pool_ch = x.shape[1] // len(POOL_WINDOWS)
    diffs = []
    for g, w in enumerate(POOL_WINDOWS):
        ch = slice(g * pool_ch, (g + 1) * pool_ch)
        tot = x[:, ch]
        for s in range(1, w):
            tot = tot + ext_ref[HALO - s:HALO - s + tm, ch]
        cnt = jnp.minimum(pos + 1, w).astype(F32)
        diffs.append(tot / cnt - x[:, ch])
    mix = _pool_mix(diffs, wp_ref, sc_ref[...])
    o_ref[...] = _layernorm(alpha * x + mix, g_ref[...], b_ref[...])


def _pool_prompt(x, wp, scale, g, b, *, seq, tm, alpha):
    m, d = x.shape
    assert seq % tm == 0 and tm % HALO == 0
    tiles_per_seq = seq // tm
    halo_blocks = tm // HALO
    return pl.pallas_call(
        functools.partial(_pool_prompt_kernel, tm=tm, tiles_per_seq=tiles_per_seq, alpha=alpha),
        grid=(m // tm,),
        in_specs=[pl.BlockSpec((tm, d), lambda i: (i, 0)),
                  pl.BlockSpec((HALO, d), lambda i: (jnp.maximum(i * halo_blocks - 1, 0), 0)),
                  _const_spec(wp.shape), _const_spec((1, d)), _const_spec((1, d)), _const_spec((1, d))],
        out_specs=pl.BlockSpec((tm, d), lambda i: (i, 0)),
        out_shape=jax.ShapeDtypeStruct((m, d), F32),
        scratch_shapes=[pltpu.VMEM((tm + HALO, d), F32)],
        compiler_params=_params(1),
    )(x, x, wp, scale.reshape(1, d), g.reshape(1, d), b.reshape(1, d))


def _pool_sample_kernel(ext_ref, wp_ref, sc_ref, g_ref, b_ref, o_ref, *, alpha):
    n_rows = ext_ref.shape[1]
    x = ext_ref[:, n_rows - 1, :]
    pool_ch = x.shape[1] // len(POOL_WINDOWS)
    diffs = []
    for g, w in enumerate(POOL_WINDOWS):
        ch = slice(g * pool_ch, (g + 1) * pool_ch)
        tot = x[:, ch]
        for s in range(1, w):
            tot = tot + ext_ref[:, n_rows - 1 - s, ch]
        diffs.append(tot / float(w) - x[:, ch])
    mix = _pool_mix(diffs, wp_ref, sc_ref[...])
    o_ref[...] = _layernorm(alpha * x + mix, g_ref[...], b_ref[...])


def _pool_sample(ext, wp, scale, g, b, *, alpha):
    bs, n_rows, d = ext.shape
    assert n_rows >= max(POOL_WINDOWS)
    return pl.pallas_call(
        functools.partial(_pool_sample_kernel, alpha=alpha),
        out_shape=jax.ShapeDtypeStruct((bs, d), F32),
        compiler_params=pltpu.CompilerParams(vmem_limit_bytes=VMEM_LIMIT_BYTES),
    )(ext, wp, scale.reshape(1, d), g.reshape(1, d), b.reshape(1, d))


def _row_tile(m, target):
    return target if m % target == 0 else m


def kernel(x_prompt, x_sample, cache_k, cache_v, cache_kidx, state_pool, page_table, ln_g, ln_b,
           ffn1_wi, ffn1_wo, ffn2_wi, ffn2_wo, attn_w_in, attn_w_o, pool_w, pool_scale):
    batch, seq, d = x_prompt.shape
    bs, dec_seq, _ = x_sample.shape
    assert dec_seq == 1
    depth = ln_g.shape[0]
    alpha = (2 * depth) ** 0.25
    past = page_table.shape[1] * PAGE_SIZE

    xp = x_prompt.reshape(batch * seq, d)
    xs = x_sample.reshape(bs, d)
    tm_p = _row_tile(batch * seq, 512)
    tq = _row_tile(seq, 256)

    def ffn(x, wi, wo, g, b, tm):
        return _ffn_ln(x, wi.astype(BF16), wo.astype(BF16), g, b, tm=tm, alpha=alpha)

    tabs_p = _rope_tables(jnp.arange(seq, dtype=jnp.int32).astype(F32))
    tabs_s = _rope_tables((past + jnp.zeros((bs,), jnp.int32)).astype(F32))

    kp_l, vp_l, kip_l, poolp_l = [], [], [], []
    ks_l, vs_l, kis_l, pools_l = [], [], [], []
    for l in range(depth):
        j = l // N_MIXERS
        xp = ffn(xp, ffn1_wi[l], ffn1_wo[l], ln_g[l, 0], ln_b[l, 0], tm_p)
        xs = ffn(xs, ffn1_wi[l], ffn1_wo[l], ln_g[l, 0], ln_b[l, 0], bs)
        if l % N_MIXERS == 0:
            w_in = attn_w_in[j].astype(BF16)
            w_in = jnp.pad(w_in, ((0, 0), (0, W_IN_PAD - w_in.shape[1])))
            w_o = attn_w_o[j].astype(BF16)
            q, k, v, qi, kiw, kab, vab, kiab = _attn_proj(xp, w_in, tabs_p, tm=tm_p, attn_layouts=True)
            xp = _dsa_prompt(q, qi, kiw, kiab, kab, vab, xp, w_o, ln_g[l, 1], ln_b[l, 1],
                             batch=batch, seq=seq, tq=tq, alpha=alpha)
            kp_l.append(k.reshape(batch, seq, N_KV_HEADS, HEAD_DIM))
            vp_l.append(v.reshape(batch, seq, N_KV_HEADS, HEAD_DIM))
            kip_l.append(kiw[:, :IDX_DIM].reshape(batch, seq, IDX_DIM))

            qs, k_s, v_s, qis, kiws = _attn_proj(xs, w_in, tabs_s, tm=bs, attn_layouts=False)
            attn_s = _dsa_sample(qs, qis, kiws, k_s, v_s, cache_k[j], cache_v[j], cache_kidx[j],
                                 page_table)
            xs = _matmul_ln(attn_s, w_o, xs, ln_g[l, 1], ln_b[l, 1], alpha=alpha)
            ks_l.append(k_s.reshape(bs, 1, N_KV_HEADS, HEAD_DIM))
            vs_l.append(v_s.reshape(bs, 1, N_KV_HEADS, HEAD_DIM))
            kis_l.append(kiws[:, :IDX_DIM].reshape(bs, 1, IDX_DIM))
        else:
            wp = pool_w[j].astype(BF16)
            poolp_l.append(xp.reshape(batch, seq, d)[:, seq - POOL_BUF:])
            ext = jnp.concatenate([state_pool[j], xs[:, None, :]], axis=1)
            pools_l.append(ext[:, ext.shape[1] - POOL_BUF:])
            xp = _pool_prompt(xp, wp, pool_scale[j], ln_g[l, 1], ln_b[l, 1],
                              seq=seq, tm=_row_tile(seq, 512), alpha=alpha)
            xs = _pool_sample(ext, wp, pool_scale[j], ln_g[l, 1], ln_b[l, 1], alpha=alpha)
        xp = ffn(xp, ffn2_wi[l], ffn2_wo[l], ln_g[l, 2], ln_b[l, 2], tm_p)
        xs = ffn(xs, ffn2_wi[l], ffn2_wo[l], ln_g[l, 2], ln_b[l, 2], bs)

    return (xp.reshape(batch, seq, d), xs.reshape(bs, 1, d),
            jnp.stack(kp_l), jnp.stack(vp_l), jnp.stack(kip_l), jnp.stack(poolp_l),
            jnp.stack(ks_l), jnp.stack(vs_l), jnp.stack(kis_l), jnp.stack(pools_l))
```

```python
import functools

import jax
import jax.numpy as jnp
from jax import lax
from jax.experimental import pallas as pl
from jax.experimental.pallas import tpu as pltpu

N_MIXERS = 2
N_HEADS = 16
HEAD_DIM = 64
N_KV_HEADS = 4
GQA_GROUP = N_HEADS // N_KV_HEADS
ROT_DIM = HEAD_DIM // 4
ROPE_THETA = 500000.0
IDX_HEADS = 8
IDX_DIM = 64
TOPK_MAX = 256
PAGE_SIZE = 128
POOL_WINDOWS = (2, 4, 8, 16)
POOL_BUF = max(POOL_WINDOWS) - 1
LN_EPS = 1e-5
Q_COLS = N_HEADS * HEAD_DIM
KV_COLS = N_KV_HEADS * HEAD_DIM
QI_COLS = IDX_HEADS * IDX_DIM

LANES = 128
SUBLANES = 8
BF16_ROWS = 16
VMEM_LIMIT_BYTES = 56 * 1024 * 1024

F32 = jnp.float32
BF16 = jnp.bfloat16
INT_MIN = -(2 ** 31)
NEG_BIG = -1e30

_NT = (((1,), (1,)), ((), ()))


def _layernorm(y, g, b):
    mu = jnp.mean(y, axis=-1, keepdims=True)
    yc = y - mu
    var = jnp.mean(yc * yc, axis=-1, keepdims=True)
    return yc * lax.rsqrt(var + LN_EPS) * g + b


def _const_spec(shape):
    nd = len(shape)
    return pl.BlockSpec(shape, lambda *_: (0,) * nd, pipeline_mode=pl.Buffered(1))


def _params(n_grid):
    return pltpu.CompilerParams(dimension_semantics=("arbitrary",) * n_grid,
                                vmem_limit_bytes=VMEM_LIMIT_BYTES)


FFN_CHUNK = 256


def _ffn_kernel(x_ref, wi_ref, wo_ref, g_ref, b_ref, o_ref, acc_ref, *, d_ff, alpha):
    x = x_ref[...]
    xb = x.astype(BF16)
    for c in range(d_ff // FFN_CHUNK):
        lo = c * FFN_CHUNK
        hg = jnp.dot(xb, wi_ref[:, lo:lo + FFN_CHUNK], preferred_element_type=F32)
        hu = jnp.dot(xb, wi_ref[:, d_ff + lo:d_ff + lo + FFN_CHUNK], preferred_element_type=F32)
        a = (hg * jax.nn.sigmoid(hg) * hu).astype(BF16)
        part = jnp.dot(a, wo_ref[lo:lo + FFN_CHUNK, :], preferred_element_type=F32)
        if c == 0:
            acc_ref[...] = part
        else:
            acc_ref[...] += part
    o_ref[...] = _layernorm(alpha * x + 0.5 * acc_ref[...], g_ref[...], b_ref[...])


def _ffn_ln(x, wi, wo, g, b, *, tm, alpha):
    m, d = x.shape
    d_ff = wo.shape[0]
    assert m % tm == 0 and d_ff % FFN_CHUNK == 0
    return pl.pallas_call(
        functools.partial(_ffn_kernel, d_ff=d_ff, alpha=alpha),
        grid=(m // tm,),
        in_specs=[pl.BlockSpec((tm, d), lambda i: (i, 0)),
                  _const_spec(wi.shape), _const_spec(wo.shape),
                  _const_spec((1, d)), _const_spec((1, d))],
        out_specs=pl.BlockSpec((tm, d), lambda i: (i, 0)),
        out_shape=jax.ShapeDtypeStruct((m, d), F32),
        scratch_shapes=[pltpu.VMEM((tm, d), F32)],
        compiler_params=_params(1),
    )(x, wi, wo, g.reshape(1, d), b.reshape(1, d))


O_K = Q_COLS
O_V = O_K + KV_COLS
O_QI = O_V + KV_COLS
O_KI = O_QI + QI_COLS
W_IN_PAD = O_KI + LANES
LOG2E = 1.4426950408889634
Q_SCALE = HEAD_DIM ** -0.5 * LOG2E


def _proj_kernel(x_ref, w_ref, cos_ref, sa_ref, sb_ref, q_ref, k_ref, v_ref, qi_ref, kiw_ref,
                 *attn_refs):
    xb = x_ref[...].astype(BF16)
    cos = cos_ref[...]
    sin_a = sa_ref[...]
    sin_b = sb_ref[...]
    tm = xb.shape[0]
    low = lax.broadcasted_iota(jnp.int32, (tm, LANES), 1) < HEAD_DIM

    def rope(t):
        return (t * cos + pltpu.roll(t, ROT_DIM // 2, 1) * sin_a
                + pltpu.roll(t, LANES - ROT_DIM // 2, 1) * sin_b)

    def proj(lo, width):
        return jnp.dot(xb, w_ref[:, lo:lo + width], preferred_element_type=F32)

    pq = proj(0, Q_COLS)
    for t in range(Q_COLS // LANES):
        q_ref[:, t * LANES:(t + 1) * LANES] = (
            rope(pq[:, t * LANES:(t + 1) * LANES]) * Q_SCALE).astype(BF16)
    pqi = proj(O_QI, QI_COLS)
    for t in range(QI_COLS // LANES):
        qi_ref[:, t * LANES:(t + 1) * LANES] = rope(pqi[:, t * LANES:(t + 1) * LANES]).astype(BF16)
    pkw = proj(O_KI, LANES)
    kw = jnp.where(low, rope(pkw), pkw)
    if attn_refs:
        kiw_ref[0] = kw.T
    else:
        kiw_ref[...] = kw

    pk = proj(O_K, KV_COLS)
    pv = proj(O_V, KV_COLS)
    if attn_refs:
        kab_ref, vab_ref, kiab_ref = attn_refs
        kiab_ref[0] = jnp.where(low, kw, 0.0).astype(BF16)
        kiab_ref[1] = jnp.where(low, 0.0, pltpu.roll(kw, HEAD_DIM, 1)).astype(BF16)
        lane = lax.broadcasted_iota(jnp.int32, (tm, LANES), 1)
        ones_a = jnp.where(lane == HEAD_DIM, 1.0, 0.0)
        ones_b = jnp.where(lane == 0, 1.0, 0.0)
    for t in range(KV_COLS // LANES):
        kr = rope(pk[:, t * LANES:(t + 1) * LANES])
        vv = pv[:, t * LANES:(t + 1) * LANES]
        if not attn_refs:
            k_ref[:, t * LANES:(t + 1) * LANES] = kr
            v_ref[:, t * LANES:(t + 1) * LANES] = vv
        else:
            g0, g1 = 2 * t, 2 * t + 1
            k_t, v_t = kr.T, vv.T
            for half, g in enumerate((g0, g1)):
                k_ref[0, g] = k_t[half * HEAD_DIM:(half + 1) * HEAD_DIM]
                v_ref[0, g] = v_t[half * HEAD_DIM:(half + 1) * HEAD_DIM]
            krr = pltpu.roll(kr, HEAD_DIM, 1)
            kab_ref[2 * g0] = jnp.where(low, kr, 0.0).astype(BF16)
            kab_ref[2 * g0 + 1] = jnp.where(low, 0.0, krr).astype(BF16)
            kab_ref[2 * g1] = jnp.where(low, krr, 0.0).astype(BF16)
            kab_ref[2 * g1 + 1] = jnp.where(low, 0.0, kr).astype(BF16)
            vvr = pltpu.roll(vv, HEAD_DIM, 1)
            vab_ref[2 * g0] = jnp.where(low, vv, ones_a).astype(BF16)
            vab_ref[2 * g0 + 1] = jnp.where(low, ones_b, vvr).astype(BF16)
            vab_ref[2 * g1] = jnp.where(low, vvr, ones_a).astype(BF16)
            vab_ref[2 * g1 + 1] = jnp.where(low, ones_b, vv).astype(BF16)


def _rope_tables(pos):
    half = ROT_DIM // 2
    freqs = ROPE_THETA ** (-jnp.arange(half, dtype=F32) / half)
    ang = pos[:, None] * freqs[None, :]
    cos, sin = jnp.cos(ang), jnp.sin(ang)
    t = pos.shape[0]
    rest = HEAD_DIM - ROT_DIM
    zeros_h = jnp.zeros((t, half), F32)
    c64 = jnp.concatenate([cos, cos, jnp.ones((t, rest), F32)], axis=1)
    a64 = jnp.concatenate([zeros_h, sin, jnp.zeros((t, rest), F32)], axis=1)
    b64 = jnp.concatenate([-sin, zeros_h, jnp.zeros((t, rest), F32)], axis=1)
    rep = LANES // HEAD_DIM
    return tuple(jnp.tile(a, (1, rep)) for a in (c64, a64, b64))


def _attn_proj(x, w_pad, tables, *, tm, attn_layouts):
    m, d = x.shape
    t_rows = tables[0].shape[0]
    assert m % tm == 0 and t_rows % tm == 0
    n_t = t_rows // tm
    row = lambda i: (i, 0)
    tab_spec = pl.BlockSpec((tm, LANES), lambda i: (i % n_t, 0))
    n_kv2 = 2 * N_KV_HEADS
    if attn_layouts:
        n_seq = m // t_rows
        kv_spec = pl.BlockSpec((1, N_KV_HEADS, HEAD_DIM, tm), lambda i: (i // n_t, 0, 0, i % n_t))
        kv_shape = jax.ShapeDtypeStruct((n_seq, N_KV_HEADS, HEAD_DIM, t_rows), F32)
        kiw_spec = pl.BlockSpec((1, LANES, tm), lambda i: (i // n_t, 0, i % n_t))
        kiw_shape = jax.ShapeDtypeStruct((n_seq, LANES, t_rows), F32)
    else:
        kv_spec = pl.BlockSpec((tm, KV_COLS), row)
        kv_shape = jax.ShapeDtypeStruct((m, KV_COLS), F32)
        kiw_spec = pl.BlockSpec((tm, LANES), row)
        kiw_shape = jax.ShapeDtypeStruct((m, LANES), F32)
    out_specs = [pl.BlockSpec((tm, Q_COLS), row), kv_spec, kv_spec, pl.BlockSpec((tm, QI_COLS), row),
                 kiw_spec]
    out_shape = [jax.ShapeDtypeStruct((m, Q_COLS), BF16), kv_shape, kv_shape,
                 jax.ShapeDtypeStruct((m, QI_COLS), BF16), kiw_shape]
    if attn_layouts:
        out_specs += [pl.BlockSpec((n_kv2, tm, LANES), lambda i: (0, i, 0)),
                      pl.BlockSpec((n_kv2, tm, LANES), lambda i: (0, i, 0)),
                      pl.BlockSpec((2, tm, LANES), lambda i: (0, i, 0))]
        out_shape += [jax.ShapeDtypeStruct((n_kv2, m, LANES), BF16),
                      jax.ShapeDtypeStruct((n_kv2, m, LANES), BF16),
                      jax.ShapeDtypeStruct((2, m, LANES), BF16)]
    return pl.pallas_call(
        _proj_kernel,
        grid=(m // tm,),
        in_specs=[pl.BlockSpec((tm, d), row), _const_spec(w_pad.shape), tab_spec, tab_spec, tab_spec],
        out_specs=out_specs,
        out_shape=out_shape,
        compiler_params=_params(1),
    )(x, w_pad, *tables)


def _order_key(score):
    score = jnp.where(score == 0.0, 0.0, score)
    bits = pltpu.bitcast(score, jnp.int32)
    return bits ^ ((bits >> 31) & 0x7FFFFFFF)


def _kth_largest(count_ge, shape, n_total, k):
    def body(it, state):
        t, n_t = state
        cand = t + lax.shift_left(jnp.int32(1), 31 - it)
        n_c = count_ge(cand)
        take = n_c >= k
        return jnp.where(take, cand, t), jnp.where(take, n_c, n_t)
    t0 = jnp.full(shape, INT_MIN, jnp.int32)
    return lax.fori_loop(0, 32, body, (t0, jnp.zeros_like(t0) + n_total))


def _dsa_prompt_kernel(q_ref, qi_ref, kiwq_ref, kiab_ref, kab_ref, vab_ref, x_ref, wo_ref,
                       g_ref, b_ref, o_ref, key_ref, acc_ref, m_ref, *, tq, topk, alpha):
    i = pl.program_id(1)
    n_chunks = i + 1
    n_pairs = (n_chunks + 1) // 2
    seq = key_ref.shape[0] - tq
    kpos = lax.broadcasted_iota(jnp.int32, (tq, tq), 0)
    qpos = lax.broadcasted_iota(jnp.int32, (tq, tq), 1)
    kpos2 = lax.broadcasted_iota(jnp.int32, (2 * tq, tq), 0)

    def chunk_off(j):
        return pl.multiple_of(j * tq, tq)

    w_idx = kiwq_ref[0, HEAD_DIM:HEAD_DIM + IDX_HEADS, :] * (IDX_DIM ** -0.5 * IDX_HEADS ** -0.5)
    key_ref[pl.ds(chunk_off(n_chunks), tq), :] = jnp.full((tq, tq), INT_MIN, jnp.int32)

    def score_chunk(j, carry):
        off = chunk_off(j)
        sc = jnp.zeros((tq, tq), F32)
        for h in range(IDX_HEADS):
            qt = qi_ref[:, (h // 2) * LANES:(h // 2 + 1) * LANES]
            s = lax.dot_general(kiab_ref[h % 2, pl.ds(off, tq), :], qt, _NT, preferred_element_type=F32)
            sc = sc + jnp.maximum(s, 0.0) * w_idx[h:h + 1]
        key = _order_key(sc)
        key = jnp.where(jnp.logical_and(j == i, kpos > qpos), INT_MIN, key)
        key_ref[pl.ds(off, tq), :] = key
        return carry

    lax.fori_loop(0, n_chunks, score_chunk, 0)

    acc_rows = 4 * SUBLANES

    def fold_rows(x):
        return jnp.sum(x.reshape(2 * tq // acc_rows, acc_rows, tq), axis=0)

    def count_keys(pred):
        def body(jp, cnt):
            off = pl.multiple_of(jp * 2 * tq, 2 * tq)
            return cnt + fold_rows(jnp.where(pred(key_ref[pl.ds(off, 2 * tq), :], off), 1, 0))
        cnt = lax.fori_loop(0, n_pairs, body, jnp.zeros((acc_rows, tq), jnp.int32))
        return jnp.sum(cnt, axis=0, keepdims=True)

    thr, n_ge = _kth_largest(lambda cand: count_keys(lambda x, off: x >= cand), (1, tq),
                             n_pairs * 2 * tq, topk)
    tie = jnp.logical_and(n_ge > topk, thr > INT_MIN)

    @pl.when(jnp.max(jnp.where(tie, 1, 0)) > 0)
    def _():
        need = topk - count_keys(lambda x, off: x > thr)
        n_bits = (seq - 1).bit_length()

        def pbody(it, p):
            cand = p + lax.shift_left(jnp.int32(1), n_bits - 1 - it)
            c = count_keys(lambda x, off: jnp.logical_and(x == thr, (kpos2 + off) < cand))
            return jnp.where(c < need, cand, p)
        last = lax.fori_loop(0, n_bits, pbody, jnp.zeros((1, tq), jnp.int32))

        def patch(j, carry):
            off = chunk_off(j)
            x = key_ref[pl.ds(off, tq), :]
            drop = jnp.logical_and(x == thr, (kpos + off) > last)
            key_ref[pl.ds(off, tq), :] = jnp.where(jnp.logical_and(tie, drop), thr - 1, x)
            return carry
        lax.fori_loop(0, n_chunks, patch, 0)

    thr = jnp.maximum(thr, INT_MIN + 1)

    acc_ref[...] = jnp.zeros_like(acc_ref)
    m_ref[...] = jnp.full_like(m_ref, NEG_BIG)
    lane_tiles = tq // LANES

    def attn_chunk(j, carry):
        off = chunk_off(j)
        bias = jnp.where(key_ref[pl.ds(off, tq), :] >= thr, 0.0, NEG_BIG).T
        for h in range(N_HEADS):
            kv = 2 * (h // GQA_GROUP) + h % 2
            qt = q_ref[:, (h // 2) * LANES:(h // 2 + 1) * LANES]
            s = lax.dot_general(qt, kab_ref[kv, pl.ds(off, tq), :], _NT,
                                preferred_element_type=F32) + bias
            m_prev = m_ref[h]
            m_next = jnp.maximum(m_prev, jnp.max(s, axis=1, keepdims=True))
            p = jnp.exp2(s - jnp.concatenate([m_next] * lane_tiles, axis=1))
            acc_ref[h] = (jnp.exp2(m_prev - m_next) * acc_ref[h]
                          + jnp.dot(p.astype(BF16), vab_ref[kv, pl.ds(off, tq), :],
                                    preferred_element_type=F32))
            m_ref[h] = m_next
        return carry

    lax.fori_loop(0, n_chunks, attn_chunk, 0)

    low = lax.broadcasted_iota(jnp.int32, (tq, LANES), 1) < HEAD_DIM
    tiles = []
    for t in range(N_HEADS // 2):
        a0 = acc_ref[2 * t]
        a1 = acc_ref[2 * t + 1]
        o0 = a0 / a0[:, HEAD_DIM:HEAD_DIM + 1]
        o1 = a1 / a1[:, 0:1]
        tiles.append(jnp.where(low, o0, o1).astype(BF16))
    attn = jnp.concatenate(tiles, axis=1)
    mix = jnp.dot(attn, wo_ref[...], preferred_element_type=F32)
    o_ref[...] = _layernorm(alpha * x_ref[...] + mix, g_ref[...], b_ref[...])


def _dsa_prompt(q, qi, kiw, kiab, kab, vab, x, wo, g, b, *, batch, seq, tq, alpha):
    m, d = x.shape
    assert seq % tq == 0 and tq % LANES == 0
    nq = seq // tq
    topk = min(TOPK_MAX, seq // 4)
    row = lambda bi, i: (bi * nq + i, 0)
    per_seq = lambda bi, i: (0, bi, 0)
    seq_spec = lambda n: pl.BlockSpec((n, seq, LANES), per_seq, pipeline_mode=pl.Buffered(1))
    return pl.pallas_call(
        functools.partial(_dsa_prompt_kernel, tq=tq, topk=topk, alpha=alpha),
        grid=(batch, nq),
        in_specs=[pl.BlockSpec((tq, Q_COLS), row),
                  pl.BlockSpec((tq, QI_COLS), row),
                  pl.BlockSpec((1, LANES, tq), lambda bi, i: (bi, 0, i)),
                  seq_spec(2), seq_spec(2 * N_KV_HEADS), seq_spec(2 * N_KV_HEADS),
                  pl.BlockSpec((tq, d), row),
                  _const_spec(wo.shape), _const_spec((1, d)), _const_spec((1, d))],
        out_specs=pl.BlockSpec((tq, d), row),
        out_shape=jax.ShapeDtypeStruct((m, d), F32),
        scratch_shapes=[pltpu.VMEM((seq + tq, tq), jnp.int32),
                        pltpu.VMEM((N_HEADS, tq, LANES), F32),
                        pltpu.VMEM((N_HEADS, tq, LANES), F32)],
        compiler_params=_params(2),
    )(q, qi, kiw, kiab, kab, vab, x, wo, g.reshape(1, d), b.reshape(1, d))


PAGES_PER_CHUNK = 16
CHUNK_TOKENS = PAGES_PER_CHUNK * PAGE_SIZE


def _chunk_cols(c):
    return pl.ds(pl.multiple_of(c * CHUNK_TOKENS, CHUNK_TOKENS), CHUNK_TOKENS)


def _page_cols(pg):
    return pl.ds(pl.multiple_of(pg * PAGE_SIZE, PAGE_SIZE), PAGE_SIZE)


def _idx_sample_kernel(pt_ref, qi_ref, widx_ref, kinew_ref, kidx_hbm, key_ref, keyself_ref,
                       kib_ref, sem, *, n_pages):
    b = pl.program_id(0)
    n_b = pl.num_programs(0)
    slot = b % 2
    n_chunks = n_pages // PAGES_PER_CHUNK

    def page_copy(sample, sl, pg):
        return pltpu.make_async_copy(kidx_hbm.at[pt_ref[sample, pg]],
                                     kib_ref.at[sl, :, _page_cols(pg)], sem.at[sl])

    def start_sample(sample, sl):
        def body(pg, c):
            page_copy(sample, sl, pg).start()
            return c
        lax.fori_loop(0, n_pages, body, 0)

    def wait_sample(sample, sl):
        def body(pg, c):
            page_copy(sample, sl, pg).wait()
            return c
        lax.fori_loop(0, n_pages, body, 0)

    @pl.when(b == 0)
    def _():
        start_sample(0, 0)

    @pl.when(b + 1 < n_b)
    def _():
        start_sample(b + 1, 1 - slot)

    wait_sample(b, slot)
    w_idx = widx_ref[0]
    qi = qi_ref[0]

    def idx_chunk(c, carry):
        s_idx = jnp.dot(qi, kib_ref[slot, :, _chunk_cols(c)].astype(BF16), preferred_element_type=F32)
        sc = jnp.sum(jnp.maximum(s_idx, 0.0) * w_idx, axis=0, keepdims=True)
        key_ref[0, pl.ds(c, 1), :] = _order_key(sc)
        return carry
    lax.fori_loop(0, n_chunks, idx_chunk, 0)

    s_self = jnp.sum(qi.astype(F32) * kinew_ref[0], axis=1, keepdims=True)
    sc_self = jnp.sum(jnp.maximum(s_self, 0.0) * w_idx, axis=0, keepdims=True)
    keyself_ref[0] = jnp.broadcast_to(_order_key(sc_self), (1, LANES))


def _select_sample_kernel(key_ref, keyself_ref, bias_ref, selself_ref, *, topk):
    bs, n_keys = key_ref.shape
    key_self = keyself_ref[:, 0:1]
    pos = lax.broadcasted_iota(jnp.int32, (bs, n_keys), 1)
    pos_self = n_keys

    def count(pred_past, pred_self):
        c = jnp.sum(jnp.where(pred_past(key_ref[...]), 1, 0), axis=1, keepdims=True)
        return c + jnp.where(pred_self, 1, 0)

    thr, n_ge = _kth_largest(lambda cand: count(lambda k: k >= cand, key_self >= cand), (bs, 1),
                             n_keys + 1, topk)
    n_gt = count(lambda k: k > thr, key_self > thr)
    need = topk - n_gt
    n_bits = pos_self.bit_length()

    def pbody(it, p):
        cand = p + lax.shift_left(jnp.int32(1), n_bits - 1 - it)
        c = count(lambda k: jnp.logical_and(k == thr, pos < cand),
                  jnp.logical_and(key_self == thr, pos_self < cand))
        return jnp.where(c < need, cand, p)
    last = lax.fori_loop(0, n_bits, pbody, jnp.zeros((bs, 1), jnp.int32))
    last = jnp.where(n_ge > topk, last, pos_self)
    key = key_ref[...]
    sel = jnp.logical_or(key > thr, jnp.logical_and(key == thr, pos <= last))
    sel_self = jnp.logical_or(key_self > thr, jnp.logical_and(key_self == thr, pos_self <= last))
    bias_ref[...] = jnp.where(sel, 0.0, NEG_BIG)
    selself_ref[...] = jnp.broadcast_to(jnp.where(sel_self, 1.0, 0.0), selself_ref.shape)


def _attn_sample_kernel(pt_ref, q_ref, knew_ref, vnew_ref, bias_ref, selself_ref, k_hbm, v_hbm, o_ref,
                        buf_ref, sem, *, n_pages):
    b = pl.program_id(0)
    n_b = pl.num_programs(0)
    n_chunks = n_pages // PAGES_PER_CHUNK

    def chunk_copies(sample, c, sl):
        cps = []
        for pg in range(PAGES_PER_CHUNK):
            page = pt_ref[sample, c * PAGES_PER_CHUNK + pg]
            cols = pl.ds(pg * PAGE_SIZE, PAGE_SIZE)
            for which, src in enumerate((k_hbm, v_hbm)):
                cps.append(pltpu.make_async_copy(src.at[page], buf_ref.at[sl, which, :, :, cols], sem.at[sl]))
        return cps

    def start_chunk(sample, c, sl):
        for cp in chunk_copies(sample, c, sl):
            cp.start()

    def wait_chunk(sample, c, sl):
        for cp in chunk_copies(sample, c, sl):
            cp.wait()

    @pl.when(b == 0)
    def _():
        start_chunk(0, 0, 0)

    head_grp = lax.broadcasted_iota(jnp.int32, (N_HEADS, 1), 0) // GQA_GROUP
    qb = q_ref[0]
    sel_self = selself_ref[0, :, 0:1] > 0.0

    def step(c, carry):
        m_run, l_run, acc = carry
        sl = c % 2

        @pl.when(c + 1 < n_chunks)
        def _():
            start_chunk(b, c + 1, 1 - sl)

        @pl.when(jnp.logical_and(c + 1 == n_chunks, b + 1 < n_b))
        def _():
            start_chunk(b + 1, 0, 1 - sl)

        wait_chunk(b, c, sl)
        s = jnp.zeros((N_HEADS, CHUNK_TOKENS), F32)
        for g in range(N_KV_HEADS):
            s_g = jnp.dot(qb, buf_ref[sl, 0, g].astype(BF16), preferred_element_type=F32)
            s = jnp.where(head_grp == g, s_g, s)
        s = s + bias_ref[0, :, _chunk_cols(c)]
        m_new = jnp.maximum(m_run, jnp.max(s, axis=1, keepdims=True))
        p = jnp.exp2(s - m_new).astype(BF16)
        scale = jnp.exp2(m_run - m_new)
        l_new = scale * l_run + jnp.sum(p.astype(F32), axis=1, keepdims=True)
        pv = jnp.zeros((N_HEADS, HEAD_DIM), F32)
        for g in range(N_KV_HEADS):
            o_g = lax.dot_general(p, buf_ref[sl, 1, g].astype(BF16), _NT, preferred_element_type=F32)
            pv = jnp.where(head_grp == g, o_g, pv)
        return m_new, l_new, scale * acc + pv

    s_self = jnp.sum(qb.astype(F32) * knew_ref[0], axis=1, keepdims=True)
    m0 = jnp.where(sel_self, s_self, NEG_BIG)
    l0 = jnp.where(sel_self, 1.0, 0.0) + jnp.zeros_like(s_self)
    _, l_fin, acc = lax.fori_loop(0, n_chunks, step, (m0, l0, l0 * vnew_ref[0]))
    o_ref[0] = acc / l_fin


def _dsa_sample(q, qi, kiw, k_new, v_new, cache_k, cache_v, cache_kidx, page_table):
    bs = q.shape[0]
    n_pages = page_table.shape[1]
    assert n_pages % (2 * PAGES_PER_CHUNK) == 0
    n_chunks = n_pages // PAGES_PER_CHUNK
    n_keys = n_pages * PAGE_SIZE
    topk = min(TOPK_MAX, (n_keys + 1) // 4)

    kidx_t = jnp.transpose(cache_kidx, (0, 2, 1))
    k_t = jnp.transpose(cache_k, (0, 2, 3, 1))
    v_t = jnp.transpose(cache_v, (0, 2, 3, 1))
    qi_h = qi.reshape(bs, IDX_HEADS, IDX_DIM)
    w_idx = (kiw[:, HEAD_DIM:HEAD_DIM + IDX_HEADS] * (IDX_DIM ** -0.5 * IDX_HEADS ** -0.5))[:, :, None]
    ki_new = jnp.broadcast_to(kiw[:, None, :IDX_DIM], (bs, IDX_HEADS, IDX_DIM))
    q_h = q.reshape(bs, N_HEADS, HEAD_DIM)
    k_rep = jnp.repeat(k_new.reshape(bs, N_KV_HEADS, HEAD_DIM), GQA_GROUP, axis=1)
    v_rep = jnp.repeat(v_new.reshape(bs, N_KV_HEADS, HEAD_DIM), GQA_GROUP, axis=1)

    per_b = lambda shape: pl.BlockSpec((1,) + shape, lambda bi, pt: (bi, 0, 0))
    any_spec = pl.BlockSpec(memory_space=pl.ANY)
    keys, key_self = pl.pallas_call(
        functools.partial(_idx_sample_kernel, n_pages=n_pages),
        grid_spec=pltpu.PrefetchScalarGridSpec(
            num_scalar_prefetch=1, grid=(bs,),
            in_specs=[per_b((IDX_HEADS, IDX_DIM)), per_b((IDX_HEADS, 1)), per_b((IDX_HEADS, IDX_DIM)),
                      any_spec],
            out_specs=[per_b((n_chunks, CHUNK_TOKENS)), per_b((1, LANES))],
            scratch_shapes=[pltpu.VMEM((2, IDX_DIM, n_keys), F32), pltpu.SemaphoreType.DMA((2,))]),
        out_shape=[jax.ShapeDtypeStruct((bs, n_chunks, CHUNK_TOKENS), jnp.int32),
                   jax.ShapeDtypeStruct((bs, 1, LANES), jnp.int32)],
        compiler_params=_params(1),
    )(page_table, qi_h, w_idx, ki_new, kidx_t)

    bias, sel_self = pl.pallas_call(
        functools.partial(_select_sample_kernel, topk=topk),
        out_shape=[jax.ShapeDtypeStruct((bs, n_keys), F32), jax.ShapeDtypeStruct((bs, LANES), F32)],
        compiler_params=pltpu.CompilerParams(vmem_limit_bytes=VMEM_LIMIT_BYTES),
    )(keys.reshape(bs, n_keys), key_self.reshape(bs, LANES))

    out = pl.pallas_call(
        functools.partial(_attn_sample_kernel, n_pages=n_pages),
        grid_spec=pltpu.PrefetchScalarGridSpec(
            num_scalar_prefetch=1, grid=(bs,),
            in_specs=[per_b((N_HEADS, HEAD_DIM)), per_b((N_HEADS, HEAD_DIM)), per_b((N_HEADS, HEAD_DIM)),
                      per_b((1, n_keys)), per_b((1, LANES)), any_spec, any_spec],
            out_specs=per_b((N_HEADS, HEAD_DIM)),
            scratch_shapes=[pltpu.VMEM((2, 2, N_KV_HEADS, HEAD_DIM, CHUNK_TOKENS), F32),
                            pltpu.SemaphoreType.DMA((2,))]),
        out_shape=jax.ShapeDtypeStruct((bs, N_HEADS, HEAD_DIM), F32),
        compiler_params=_params(1),
    )(page_table, q_h, k_rep, v_rep, bias.reshape(bs, 1, n_keys), sel_self.reshape(bs, 1, LANES), k_t, v_t)
    return out.reshape(bs, Q_COLS)


def _matmul_ln_kernel(a_ref, w_ref, x_ref, g_ref, b_ref, o_ref, *, alpha):
    mix = jnp.dot(a_ref[...].astype(BF16), w_ref[...], preferred_element_type=F32)
    o_ref[...] = _layernorm(alpha * x_ref[...] + mix, g_ref[...], b_ref[...])


def _matmul_ln(a, w, x, g, b, *, alpha):
    m, d = x.shape
    return pl.pallas_call(
        functools.partial(_matmul_ln_kernel, alpha=alpha),
        out_shape=jax.ShapeDtypeStruct((m, d), F32),
        compiler_params=pltpu.CompilerParams(vmem_limit_bytes=VMEM_LIMIT_BYTES),
    )(a, w, x, g.reshape(1, d), b.reshape(1, d))


HALO = 16


def _pool_mix(diffs, wp_ref, scale):
    outs = [jnp.dot(d.astype(BF16), wp_ref[g], preferred_element_type=F32) for g, d in enumerate(diffs)]
    return jnp.concatenate(outs, axis=1) * scale


def _pool_prompt_kernel(x_ref, halo_ref, wp_ref, sc_ref, g_ref, b_ref, o_ref, ext_ref,
                        *, tm, tiles_per_seq, alpha):
    i = pl.program_id(0)
    t_in_seq = i % tiles_per_seq
    x = x_ref[...]
    ext_ref[HALO:, :] = x
    ext_ref[:HALO, :] = jnp.where(t_in_seq == 0, 0.0, halo_ref[...])
    pos = t_in_seq * tm + lax.broadcasted_iota(jnp.int32, (tm, 1), 0)
    pool_ch = x.shape[1] // len(POOL_WINDOWS)
    diffs = []
    for g, w in enumerate(POOL_WINDOWS):
        ch = slice(g * pool_ch, (g + 1) * pool_ch)
        tot = x[:, ch]
        for s in range(1, w):
            tot = tot + ext_ref[HALO - s:HALO - s + tm, ch]
        cnt = jnp.minimum(pos + 1, w).astype(F32)
        diffs.append(tot / cnt - x[:, ch])
    mix = _pool_mix(diffs, wp_ref, sc_ref[...])
    o_ref[...] = _layernorm(alpha * x + mix, g_ref[...], b_ref[...])


def _pool_prompt(x, wp, scale, g, b, *, seq, tm, alpha):
    m, d = x.shape
    assert seq % tm == 0 and tm % HALO == 0
    tiles_per_seq = seq // tm
    halo_blocks = tm // HALO
    return pl.pallas_call(
        functools.partial(_pool_prompt_kernel, tm=tm, tiles_per_seq=tiles_per_seq, alpha=alpha),
        grid=(m // tm,),
        in_specs=[pl.BlockSpec((tm, d), lambda i: (i, 0)),
                  pl.BlockSpec((HALO, d), lambda i: (jnp.maximum(i * halo_blocks - 1, 0), 0)),
                  _const_spec(wp.shape), _const_spec((1, d)), _const_spec((1, d)), _const_spec((1, d))],
        out_specs=pl.BlockSpec((tm, d), lambda i: (i, 0)),
        out_shape=jax.ShapeDtypeStruct((m, d), F32),
        scratch_shapes=[pltpu.VMEM((tm + HALO, d), F32)],
        compiler_params=_params(1),
    )(x, x, wp, scale.reshape(1, d), g.reshape(1, d), b.reshape(1, d))


def _pool_sample_kernel(ext_ref, wp_ref, sc_ref, g_ref, b_ref, o_ref, *, alpha):
    n_rows = ext_ref.shape[1]
    x = ext_ref[:, n_rows - 1, :]
    pool_ch = x.shape[1] // len(POOL_WINDOWS)
    diffs = []
    for g, w in enumerate(POOL_WINDOWS):
        ch = slice(g * pool_ch, (g + 1) * pool_ch)
        tot = x[:, ch]
        for s in range(1, w):
            tot = tot + ext_ref[:, n_rows - 1 - s, ch]
        diffs.append(tot / float(w) - x[:, ch])
    mix = _pool_mix(diffs, wp_ref, sc_ref[...])
    o_ref[...] = _layernorm(alpha * x + mix, g_ref[...], b_ref[...])


def _pool_sample(ext, wp, scale, g, b, *, alpha):
    bs, n_rows, d = ext.shape
    assert n_rows >= max(POOL_WINDOWS)
    return pl.pallas_call(
        functools.partial(_pool_sample_kernel, alpha=alpha),
        out_shape=jax.ShapeDtypeStruct((bs, d), F32),
        compiler_params=pltpu.CompilerParams(vmem_limit_bytes=VMEM_LIMIT_BYTES),
    )(ext, wp, scale.reshape(1, d), g.reshape(1, d), b.reshape(1, d))


def _row_tile(m, target):
    return target if m % target == 0 else m


def kernel(x_prompt, x_sample, cache_k, cache_v, cache_kidx, state_pool, page_table, ln_g, ln_b,
           ffn1_wi, ffn1_wo, ffn2_wi, ffn2_wo, attn_w_in, attn_w_o, pool_w, pool_scale):
    batch, seq, d = x_prompt.shape
    bs, dec_seq, _ = x_sample.shape
    assert dec_seq == 1
    depth = ln_g.shape[0]
    alpha = (2 * depth) ** 0.25
    past = page_table.shape[1] * PAGE_SIZE

    xp = x_prompt.reshape(batch * seq, d)
    xs = x_sample.reshape(bs, d)
    tm_p = _row_tile(batch * seq, 512)
    tq = _row_tile(seq, 256)

    def ffn(x, wi, wo, g, b, tm):
        return _ffn_ln(x, wi.astype(BF16), wo.astype(BF16), g, b, tm=tm, alpha=alpha)

    tabs_p = _rope_tables(jnp.arange(seq, dtype=jnp.int32).astype(F32))
    tabs_s = _rope_tables((past + jnp.zeros((bs,), jnp.int32)).astype(F32))

    kp_l, vp_l, kip_l, poolp_l = [], [], [], []
    ks_l, vs_l, kis_l, pools_l = [], [], [], []
    for l in range(depth):
        j = l // N_MIXERS
        xp = ffn(xp, ffn1_wi[l], ffn1_wo[l], ln_g[l, 0], ln_b[l, 0], tm_p)
        xs = ffn(xs, ffn1_wi[l], ffn1_wo[l], ln_g[l, 0], ln_b[l, 0], bs)
        if l % N_MIXERS == 0:
            w_in = attn_w_in[j].astype(BF16)
            w_in = jnp.pad(w_in, ((0, 0), (0, W_IN_PAD - w_in.shape[1])))
            w_o = attn_w_o[j].astype(BF16)
            q, k_t, v_t, qi, kiw_t, kab, vab, kiab = _attn_proj(xp, w_in, tabs_p, tm=_row_tile(seq, 512),
                                                                attn_layouts=True)
            xp = _dsa_prompt(q, qi, kiw_t, kiab, kab, vab, xp, w_o, ln_g[l, 1], ln_b[l, 1],
                             batch=batch, seq=seq, tq=tq, alpha=alpha)
            kp_l.append(jnp.transpose(k_t, (0, 3, 1, 2)))
            vp_l.append(jnp.transpose(v_t, (0, 3, 1, 2)))
            kip_l.append(jnp.transpose(kiw_t[:, :IDX_DIM, :], (0, 2, 1)))

            qs, k_s, v_s, qis, kiws = _attn_proj(xs, w_in, tabs_s, tm=bs, attn_layouts=False)
            attn_s = _dsa_sample(qs, qis, kiws, k_s, v_s, cache_k[j], cache_v[j], cache_kidx[j],
                                 page_table)
            xs = _matmul_ln(attn_s, w_o, xs, ln_g[l, 1], ln_b[l, 1], alpha=alpha)
            ks_l.append(k_s.reshape(bs, 1, N_KV_HEADS, HEAD_DIM))
            vs_l.append(v_s.reshape(bs, 1, N_KV_HEADS, HEAD_DIM))
            kis_l.append(kiws[:, :IDX_DIM].reshape(bs, 1, IDX_DIM))
        else:
            wp = pool_w[j].astype(BF16)
            poolp_l.append(xp.reshape(batch, seq, d)[:, seq - POOL_BUF:])
            ext = jnp.concatenate([state_pool[j], xs[:, None, :]], axis=1)
            pools_l.append(ext[:, ext.shape[1] - POOL_BUF:])
            xp = _pool_prompt(xp, wp, pool_scale[j], ln_g[l, 1], ln_b[l, 1],
                              seq=seq, tm=_row_tile(seq, 512), alpha=alpha)
            xs = _pool_sample(ext, wp, pool_scale[j], ln_g[l, 1], ln_b[l, 1], alpha=alpha)
        xp = ffn(xp, ffn2_wi[l], ffn2_wo[l], ln_g[l, 2], ln_b[l, 2], tm_p)
        xs = ffn(xs, ffn2_wi[l], ffn2_wo[l], ln_g[l, 2], ln_b[l, 2], bs)

    return (xp.reshape(batch, seq, d), xs.reshape(bs, 1, d),
            jnp.stack(kp_l), jnp.stack(vp_l), jnp.stack(kip_l), jnp.stack(poolp_l),
            jnp.stack(ks_l), jnp.stack(vs_l), jnp.stack(kis_l), jnp.stack(pools_l))
```

```python
import functools

import jax
import jax.numpy as jnp
from jax import lax
from jax.experimental import pallas as pl
from jax.experimental.pallas import tpu as pltpu

N_MIXERS = 2
N_HEADS = 16
HEAD_DIM = 64
N_KV_HEADS = 4
GQA_GROUP = N_HEADS // N_KV_HEADS
ROT_DIM = HEAD_DIM // 4
ROPE_THETA = 500000.0
IDX_HEADS = 8
IDX_DIM = 64
TOPK_MAX = 256
PAGE_SIZE = 128
POOL_WINDOWS = (2, 4, 8, 16)
POOL_BUF = max(POOL_WINDOWS) - 1
LN_EPS = 1e-5
Q_COLS = N_HEADS * HEAD_DIM
KV_COLS = N_KV_HEADS * HEAD_DIM
QI_COLS = IDX_HEADS * IDX_DIM

LANES = 128
SUBLANES = 8
BF16_ROWS = 16
VMEM_LIMIT_BYTES = 56 * 1024 * 1024

F32 = jnp.float32
BF16 = jnp.bfloat16
INT_MIN = -(2 ** 31)
NEG_BIG = -1e30

_NT = (((1,), (1,)), ((), ()))


def _layernorm(y, g, b):
    mu = jnp.mean(y, axis=-1, keepdims=True)
    yc = y - mu
    var = jnp.mean(yc * yc, axis=-1, keepdims=True)
    return yc * lax.rsqrt(var + LN_EPS) * g + b


def _const_spec(shape):
    nd = len(shape)
    return pl.BlockSpec(shape, lambda *_: (0,) * nd, pipeline_mode=pl.Buffered(1))


def _params(n_grid):
    return pltpu.CompilerParams(dimension_semantics=("arbitrary",) * n_grid,
                                vmem_limit_bytes=VMEM_LIMIT_BYTES)


FFN_CHUNK = 256


def _ffn_kernel(x_ref, wi_ref, wo_ref, g_ref, b_ref, o_ref, acc_ref, *, d_ff, alpha):
    x = x_ref[...]
    xb = x.astype(BF16)
    for c in range(d_ff // FFN_CHUNK):
        lo = c * FFN_CHUNK
        hg = jnp.dot(xb, wi_ref[:, lo:lo + FFN_CHUNK], preferred_element_type=F32)
        hu = jnp.dot(xb, wi_ref[:, d_ff + lo:d_ff + lo + FFN_CHUNK], preferred_element_type=F32)
        a = (hg * jax.nn.sigmoid(hg) * hu).astype(BF16)
        part = jnp.dot(a, wo_ref[lo:lo + FFN_CHUNK, :], preferred_element_type=F32)
        if c == 0:
            acc_ref[...] = part
        else:
            acc_ref[...] += part
    o_ref[...] = _layernorm(alpha * x + 0.5 * acc_ref[...], g_ref[...], b_ref[...])


def _ffn_ln(x, wi, wo, g, b, *, tm, alpha):
    m, d = x.shape
    d_ff = wo.shape[0]
    assert m % tm == 0 and d_ff % FFN_CHUNK == 0
    return pl.pallas_call(
        functools.partial(_ffn_kernel, d_ff=d_ff, alpha=alpha),
        grid=(m // tm,),
        in_specs=[pl.BlockSpec((tm, d), lambda i: (i, 0)),
                  _const_spec(wi.shape), _const_spec(wo.shape),
                  _const_spec((1, d)), _const_spec((1, d))],
        out_specs=pl.BlockSpec((tm, d), lambda i: (i, 0)),
        out_shape=jax.ShapeDtypeStruct((m, d), F32),
        scratch_shapes=[pltpu.VMEM((tm, d), F32)],
        compiler_params=_params(1),
    )(x, wi, wo, g.reshape(1, d), b.reshape(1, d))


O_K = Q_COLS
O_V = O_K + KV_COLS
O_QI = O_V + KV_COLS
O_KI = O_QI + QI_COLS
W_IN_PAD = O_KI + LANES
LOG2E = 1.4426950408889634
Q_SCALE = HEAD_DIM ** -0.5 * LOG2E


def _proj_kernel(x_ref, w_ref, cos_ref, sa_ref, sb_ref, q_ref, k_ref, v_ref, qi_ref, kiw_ref,
                 *attn_refs):
    xb = x_ref[...].astype(BF16)
    cos = cos_ref[...]
    sin_a = sa_ref[...]
    sin_b = sb_ref[...]
    tm = xb.shape[0]
    low = lax.broadcasted_iota(jnp.int32, (tm, LANES), 1) < HEAD_DIM

    def rope(t):
        return (t * cos + pltpu.roll(t, ROT_DIM // 2, 1) * sin_a
                + pltpu.roll(t, LANES - ROT_DIM // 2, 1) * sin_b)

    def proj(lo, width):
        return jnp.dot(xb, w_ref[:, lo:lo + width], preferred_element_type=F32)

    pq = proj(0, Q_COLS)
    for t in range(Q_COLS // LANES):
        q_ref[:, t * LANES:(t + 1) * LANES] = (
            rope(pq[:, t * LANES:(t + 1) * LANES]) * Q_SCALE).astype(BF16)
    pqi = proj(O_QI, QI_COLS)
    for t in range(QI_COLS // LANES):
        qi_ref[:, t * LANES:(t + 1) * LANES] = rope(pqi[:, t * LANES:(t + 1) * LANES]).astype(BF16)
    pkw = proj(O_KI, LANES)
    kw = jnp.where(low, rope(pkw), pkw)
    if attn_refs:
        kiw_ref[0] = kw.T
    else:
        kiw_ref[...] = kw

    pk = proj(O_K, KV_COLS)
    pv = proj(O_V, KV_COLS)
    if attn_refs:
        kab_ref, vab_ref, kiab_ref = attn_refs
        kiab_ref[0] = jnp.where(low, kw, 0.0).astype(BF16)
        kiab_ref[1] = jnp.where(low, 0.0, pltpu.roll(kw, HEAD_DIM, 1)).astype(BF16)
        lane = lax.broadcasted_iota(jnp.int32, (tm, LANES), 1)
        ones_a = jnp.where(lane == HEAD_DIM, 1.0, 0.0)
        ones_b = jnp.where(lane == 0, 1.0, 0.0)
    for t in range(KV_COLS // LANES):
        kr = rope(pk[:, t * LANES:(t + 1) * LANES])
        vv = pv[:, t * LANES:(t + 1) * LANES]
        if not attn_refs:
            k_ref[:, t * LANES:(t + 1) * LANES] = kr
            v_ref[:, t * LANES:(t + 1) * LANES] = vv
        else:
            g0, g1 = 2 * t, 2 * t + 1
            k_t, v_t = kr.T, vv.T
            for half, g in enumerate((g0, g1)):
                k_ref[0, g] = k_t[half * HEAD_DIM:(half + 1) * HEAD_DIM]
                v_ref[0, g] = v_t[half * HEAD_DIM:(half + 1) * HEAD_DIM]
            krr = pltpu.roll(kr, HEAD_DIM, 1)
            kab_ref[2 * g0] = jnp.where(low, kr, 0.0).astype(BF16)
            kab_ref[2 * g0 + 1] = jnp.where(low, 0.0, krr).astype(BF16)
            kab_ref[2 * g1] = jnp.where(low, krr, 0.0).astype(BF16)
            kab_ref[2 * g1 + 1] = jnp.where(low, 0.0, kr).astype(BF16)
            vvr = pltpu.roll(vv, HEAD_DIM, 1)
            vab_ref[2 * g0] = jnp.where(low, vv, ones_a).astype(BF16)
            vab_ref[2 * g0 + 1] = jnp.where(low, ones_b, vvr).astype(BF16)
            vab_ref[2 * g1] = jnp.where(low, vvr, ones_a).astype(BF16)
            vab_ref[2 * g1 + 1] = jnp.where(low, ones_b, vv).astype(BF16)


def _rope_tables(pos):
    half = ROT_DIM // 2
    freqs = ROPE_THETA ** (-jnp.arange(half, dtype=F32) / half)
    ang = pos[:, None] * freqs[None, :]
    cos, sin = jnp.cos(ang), jnp.sin(ang)
    t = pos.shape[0]
    rest = HEAD_DIM - ROT_DIM
    zeros_h = jnp.zeros((t, half), F32)
    c64 = jnp.concatenate([cos, cos, jnp.ones((t, rest), F32)], axis=1)
    a64 = jnp.concatenate([zeros_h, sin, jnp.zeros((t, rest), F32)], axis=1)
    b64 = jnp.concatenate([-sin, zeros_h, jnp.zeros((t, rest), F32)], axis=1)
    rep = LANES // HEAD_DIM
    return tuple(jnp.tile(a, (1, rep)) for a in (c64, a64, b64))


def _attn_proj(x, w_pad, tables, *, tm, attn_layouts):
    m, d = x.shape
    t_rows = tables[0].shape[0]
    assert m % tm == 0 and t_rows % tm == 0
    n_t = t_rows // tm
    row = lambda i: (i, 0)
    tab_spec = pl.BlockSpec((tm, LANES), lambda i: (i % n_t, 0))
    n_kv2 = 2 * N_KV_HEADS
    if attn_layouts:
        n_seq = m // t_rows
        kv_spec = pl.BlockSpec((1, N_KV_HEADS, HEAD_DIM, tm), lambda i: (i // n_t, 0, 0, i % n_t))
        kv_shape = jax.ShapeDtypeStruct((n_seq, N_KV_HEADS, HEAD_DIM, t_rows), F32)
        kiw_spec = pl.BlockSpec((1, LANES, tm), lambda i: (i // n_t, 0, i % n_t))
        kiw_shape = jax.ShapeDtypeStruct((n_seq, LANES, t_rows), F32)
    else:
        kv_spec = pl.BlockSpec((tm, KV_COLS), row)
        kv_shape = jax.ShapeDtypeStruct((m, KV_COLS), F32)
        kiw_spec = pl.BlockSpec((tm, LANES), row)
        kiw_shape = jax.ShapeDtypeStruct((m, LANES), F32)
    out_specs = [pl.BlockSpec((tm, Q_COLS), row), kv_spec, kv_spec, pl.BlockSpec((tm, QI_COLS), row),
                 kiw_spec]
    out_shape = [jax.ShapeDtypeStruct((m, Q_COLS), BF16), kv_shape, kv_shape,
                 jax.ShapeDtypeStruct((m, QI_COLS), BF16), kiw_shape]
    if attn_layouts:
        out_specs += [pl.BlockSpec((n_kv2, tm, LANES), lambda i: (0, i, 0)),
                      pl.BlockSpec((n_kv2, tm, LANES), lambda i: (0, i, 0)),
                      pl.BlockSpec((2, tm, LANES), lambda i: (0, i, 0))]
        out_shape += [jax.ShapeDtypeStruct((n_kv2, m, LANES), BF16),
                      jax.ShapeDtypeStruct((n_kv2, m, LANES), BF16),
                      jax.ShapeDtypeStruct((2, m, LANES), BF16)]
    return pl.pallas_call(
        _proj_kernel,
        grid=(m // tm,),
        in_specs=[pl.BlockSpec((tm, d), row), _const_spec(w_pad.shape), tab_spec, tab_spec, tab_spec],
        out_specs=out_specs,
        out_shape=out_shape,
        compiler_params=_params(1),
    )(x, w_pad, *tables)


def _order_key(score):
    score = jnp.where(score == 0.0, 0.0, score)
    bits = pltpu.bitcast(score, jnp.int32)
    return bits ^ ((bits >> 31) & 0x7FFFFFFF)


def _kth_largest(count_ge, shape, n_total, k):
    def body(it, state):
        t, n_t = state
        cand = t + lax.shift_left(jnp.int32(1), 31 - it)
        n_c = count_ge(cand)
        take = n_c >= k
        return jnp.where(take, cand, t), jnp.where(take, n_c, n_t)
    t0 = jnp.full(shape, INT_MIN, jnp.int32)
    return lax.fori_loop(0, 32, body, (t0, jnp.zeros_like(t0) + n_total))


def _dsa_prompt_kernel(q_ref, qi_ref, kiwq_ref, kiab_ref, kab_ref, vab_ref, x_ref, wo_ref,
                       g_ref, b_ref, o_ref, key_ref, acc_ref, m_ref, *, tq, topk, alpha):
    i = pl.program_id(1)
    n_chunks = i + 1
    n_pairs = (n_chunks + 1) // 2
    seq = key_ref.shape[0] - tq
    kpos = lax.broadcasted_iota(jnp.int32, (tq, tq), 0)
    qpos = lax.broadcasted_iota(jnp.int32, (tq, tq), 1)
    kpos2 = lax.broadcasted_iota(jnp.int32, (2 * tq, tq), 0)

    def chunk_off(j):
        return pl.multiple_of(j * tq, tq)

    w_idx = kiwq_ref[0, HEAD_DIM:HEAD_DIM + IDX_HEADS, :] * (IDX_DIM ** -0.5 * IDX_HEADS ** -0.5)
    key_ref[pl.ds(chunk_off(n_chunks), tq), :] = jnp.full((tq, tq), INT_MIN, jnp.int32)

    def score_chunk(j, carry):
        off = chunk_off(j)
        sc = jnp.zeros((tq, tq), F32)
        for h in range(IDX_HEADS):
            qt = qi_ref[:, (h // 2) * LANES:(h // 2 + 1) * LANES]
            s = lax.dot_general(kiab_ref[h % 2, pl.ds(off, tq), :], qt, _NT, preferred_element_type=F32)
            sc = sc + jnp.maximum(s, 0.0) * w_idx[h:h + 1]
        key = _order_key(sc)
        key = jnp.where(jnp.logical_and(j == i, kpos > qpos), INT_MIN, key)
        key_ref[pl.ds(off, tq), :] = key
        return carry

    def two_chunks_per_step(chunk_fn):
        def pair(jp, carry):
            chunk_fn(2 * jp, carry)
            return chunk_fn(2 * jp + 1, carry)
        lax.fori_loop(0, n_chunks // 2, pair, 0)

        @pl.when(n_chunks % 2 == 1)
        def _():
            chunk_fn(n_chunks - 1, 0)

    two_chunks_per_step(score_chunk)

    acc_rows = 4 * SUBLANES

    def fold_rows(x):
        return jnp.sum(x.reshape(2 * tq // acc_rows, acc_rows, tq), axis=0)

    def count_keys(pred):
        def body(jp, cnt):
            off = pl.multiple_of(jp * 2 * tq, 2 * tq)
            return cnt + fold_rows(jnp.where(pred(key_ref[pl.ds(off, 2 * tq), :], off), 1, 0))
        cnt = lax.fori_loop(0, n_pairs, body, jnp.zeros((acc_rows, tq), jnp.int32))
        return jnp.sum(cnt, axis=0, keepdims=True)

    thr, n_ge = _kth_largest(lambda cand: count_keys(lambda x, off: x >= cand), (1, tq),
                             n_pairs * 2 * tq, topk)
    tie = jnp.logical_and(n_ge > topk, thr > INT_MIN)

    @pl.when(jnp.max(jnp.where(tie, 1, 0)) > 0)
    def _():
        need = topk - count_keys(lambda x, off: x > thr)
        n_bits = (seq - 1).bit_length()

        def pbody(it, p):
            cand = p + lax.shift_left(jnp.int32(1), n_bits - 1 - it)
            c = count_keys(lambda x, off: jnp.logical_and(x == thr, (kpos2 + off) < cand))
            return jnp.where(c < need, cand, p)
        last = lax.fori_loop(0, n_bits, pbody, jnp.zeros((1, tq), jnp.int32))

        def patch(j, carry):
            off = chunk_off(j)
            x = key_ref[pl.ds(off, tq), :]
            drop = jnp.logical_and(x == thr, (kpos + off) > last)
            key_ref[pl.ds(off, tq), :] = jnp.where(jnp.logical_and(tie, drop), thr - 1, x)
            return carry
        lax.fori_loop(0, n_chunks, patch, 0)

    thr = jnp.maximum(thr, INT_MIN + 1)

    acc_ref[...] = jnp.zeros_like(acc_ref)
    m_ref[...] = jnp.full_like(m_ref, NEG_BIG)
    lane_tiles = tq // LANES

    def attn_chunk(j, carry):
        off = chunk_off(j)
        bias = jnp.where(key_ref[pl.ds(off, tq), :] >= thr, 0.0, NEG_BIG).T
        for h in range(N_HEADS):
            kv = 2 * (h // GQA_GROUP) + h % 2
            qt = q_ref[:, (h // 2) * LANES:(h // 2 + 1) * LANES]
            s = lax.dot_general(qt, kab_ref[kv, pl.ds(off, tq), :], _NT,
                                preferred_element_type=F32) + bias
            m_prev = m_ref[h]
            m_next = jnp.maximum(m_prev, jnp.max(s, axis=1, keepdims=True))
            p = jnp.exp2(s - jnp.concatenate([m_next] * lane_tiles, axis=1))
            acc_ref[h] = (jnp.exp2(m_prev - m_next) * acc_ref[h]
                          + jnp.dot(p.astype(BF16), vab_ref[kv, pl.ds(off, tq), :],
                                    preferred_element_type=F32))
            m_ref[h] = m_next
        return carry

    two_chunks_per_step(attn_chunk)

    low = lax.broadcasted_iota(jnp.int32, (tq, LANES), 1) < HEAD_DIM
    tiles = []
    for t in range(N_HEADS // 2):
        a0 = acc_ref[2 * t]
        a1 = acc_ref[2 * t + 1]
        o0 = a0 / a0[:, HEAD_DIM:HEAD_DIM + 1]
        o1 = a1 / a1[:, 0:1]
        tiles.append(jnp.where(low, o0, o1).astype(BF16))
    attn = jnp.concatenate(tiles, axis=1)
    mix = jnp.dot(attn, wo_ref[...], preferred_element_type=F32)
    o_ref[...] = _layernorm(alpha * x_ref[...] + mix, g_ref[...], b_ref[...])


def _dsa_prompt(q, qi, kiw, kiab, kab, vab, x, wo, g, b, *, batch, seq, tq, alpha):
    m, d = x.shape
    assert seq % tq == 0 and tq % LANES == 0
    nq = seq // tq
    topk = min(TOPK_MAX, seq // 4)
    row = lambda bi, i: (bi * nq + i, 0)
    per_seq = lambda bi, i: (0, bi, 0)
    seq_spec = lambda n: pl.BlockSpec((n, seq, LANES), per_seq, pipeline_mode=pl.Buffered(1))
    return pl.pallas_call(
        functools.partial(_dsa_prompt_kernel, tq=tq, topk=topk, alpha=alpha),
        grid=(batch, nq),
        in_specs=[pl.BlockSpec((tq, Q_COLS), row),
                  pl.BlockSpec((tq, QI_COLS), row),
                  pl.BlockSpec((1, LANES, tq), lambda bi, i: (bi, 0, i)),
                  seq_spec(2), seq_spec(2 * N_KV_HEADS), seq_spec(2 * N_KV_HEADS),
                  pl.BlockSpec((tq, d), row),
                  _const_spec(wo.shape), _const_spec((1, d)), _const_spec((1, d))],
        out_specs=pl.BlockSpec((tq, d), row),
        out_shape=jax.ShapeDtypeStruct((m, d), F32),
        scratch_shapes=[pltpu.VMEM((seq + tq, tq), jnp.int32),
                        pltpu.VMEM((N_HEADS, tq, LANES), F32),
                        pltpu.VMEM((N_HEADS, tq, LANES), F32)],
        compiler_params=_params(2),
    )(q, qi, kiw, kiab, kab, vab, x, wo, g.reshape(1, d), b.reshape(1, d))


PAGES_PER_CHUNK = 16
KV_SLOTS = 3
CHUNK_TOKENS = PAGES_PER_CHUNK * PAGE_SIZE


def _chunk_cols(c):
    return pl.ds(pl.multiple_of(c * CHUNK_TOKENS, CHUNK_TOKENS), CHUNK_TOKENS)


def _page_cols(pg):
    return pl.ds(pl.multiple_of(pg * PAGE_SIZE, PAGE_SIZE), PAGE_SIZE)


def _idx_sample_kernel(pt_ref, qi_ref, widx_ref, kinew_ref, kidx_hbm, key_ref, keyself_ref,
                       kib_ref, sem, *, n_pages):
    b = pl.program_id(0)
    n_b = pl.num_programs(0)
    slot = b % 2
    n_chunks = n_pages // PAGES_PER_CHUNK

    def page_copy(sample, sl, pg):
        return pltpu.make_async_copy(kidx_hbm.at[pt_ref[sample, pg]],
                                     kib_ref.at[sl, :, _page_cols(pg)], sem.at[sl])

    def start_sample(sample, sl):
        def body(pg, c):
            page_copy(sample, sl, pg).start()
            return c
        lax.fori_loop(0, n_pages, body, 0)

    def wait_sample(sample, sl):
        def body(pg, c):
            page_copy(sample, sl, pg).wait()
            return c
        lax.fori_loop(0, n_pages, body, 0)

    @pl.when(b == 0)
    def _():
        start_sample(0, 0)

    @pl.when(b + 1 < n_b)
    def _():
        start_sample(b + 1, 1 - slot)

    wait_sample(b, slot)
    w_idx = widx_ref[0]
    qi = qi_ref[0]

    def idx_chunk(c, carry):
        s_idx = jnp.dot(qi, kib_ref[slot, :, _chunk_cols(c)].astype(BF16), preferred_element_type=F32)
        sc = jnp.sum(jnp.maximum(s_idx, 0.0) * w_idx, axis=0, keepdims=True)
        key_ref[0, pl.ds(c, 1), :] = _order_key(sc)
        return carry
    lax.fori_loop(0, n_chunks, idx_chunk, 0)

    s_self = jnp.sum(qi.astype(F32) * kinew_ref[0], axis=1, keepdims=True)
    sc_self = jnp.sum(jnp.maximum(s_self, 0.0) * w_idx, axis=0, keepdims=True)
    keyself_ref[0] = jnp.broadcast_to(_order_key(sc_self), (1, LANES))


def _select_sample_kernel(key_ref, keyself_ref, bias_ref, selself_ref, *, topk):
    bs, n_keys = key_ref.shape
    key_self = keyself_ref[:, 0:1]
    pos = lax.broadcasted_iota(jnp.int32, (bs, n_keys), 1)
    pos_self = n_keys

    def count(pred_past, pred_self):
        c = jnp.sum(jnp.where(pred_past(key_ref[...]), 1, 0), axis=1, keepdims=True)
        return c + jnp.where(pred_self, 1, 0)

    thr, n_ge = _kth_largest(lambda cand: count(lambda k: k >= cand, key_self >= cand), (bs, 1),
                             n_keys + 1, topk)
    n_gt = count(lambda k: k > thr, key_self > thr)
    need = topk - n_gt
    n_bits = pos_self.bit_length()

    def pbody(it, p):
        cand = p + lax.shift_left(jnp.int32(1), n_bits - 1 - it)
        c = count(lambda k: jnp.logical_and(k == thr, pos < cand),
                  jnp.logical_and(key_self == thr, pos_self < cand))
        return jnp.where(c < need, cand, p)
    last = lax.fori_loop(0, n_bits, pbody, jnp.zeros((bs, 1), jnp.int32))
    last = jnp.where(n_ge > topk, last, pos_self)
    key = key_ref[...]
    sel = jnp.logical_or(key > thr, jnp.logical_and(key == thr, pos <= last))
    sel_self = jnp.logical_or(key_self > thr, jnp.logical_and(key_self == thr, pos_self <= last))
    bias_ref[...] = jnp.where(sel, 0.0, NEG_BIG)
    selself_ref[...] = jnp.broadcast_to(jnp.where(sel_self, 1.0, 0.0), selself_ref.shape)


def _attn_sample_kernel(pt_ref, q_ref, knew_ref, vnew_ref, bias_ref, selself_ref, k_hbm, v_hbm, o_ref,
                        buf_ref, sem, *, n_pages):
    b = pl.program_id(0)
    n_b = pl.num_programs(0)
    n_chunks = n_pages // PAGES_PER_CHUNK
    first = b * n_chunks

    def chunk_copies(sample, c, sl):
        cps = []
        for pg in range(PAGES_PER_CHUNK):
            page = pt_ref[sample, c * PAGES_PER_CHUNK + pg]
            cols = pl.ds(pg * PAGE_SIZE, PAGE_SIZE)
            for which, src in enumerate((k_hbm, v_hbm)):
                cps.append(pltpu.make_async_copy(src.at[page], buf_ref.at[sl, which, :, :, cols], sem.at[sl]))
        return cps

    def start_chunk(sample, c, sl):
        for cp in chunk_copies(sample, c, sl):
            cp.start()

    def wait_chunk(sample, c, sl):
        for cp in chunk_copies(sample, c, sl):
            cp.wait()

    @pl.when(b == 0)
    def _():
        for c in range(KV_SLOTS - 1):
            start_chunk(0, c, c)

    head_grp = lax.broadcasted_iota(jnp.int32, (N_HEADS, 1), 0) // GQA_GROUP
    qb = q_ref[0]
    sel_self = selself_ref[0, :, 0:1] > 0.0

    def step(c, carry):
        m_run, l_run, acc = carry
        sl = (first + c) % KV_SLOTS
        ahead = c + KV_SLOTS - 1
        sl_ahead = (first + ahead) % KV_SLOTS

        @pl.when(ahead < n_chunks)
        def _():
            start_chunk(b, ahead, sl_ahead)

        @pl.when(jnp.logical_and(ahead >= n_chunks, b + 1 < n_b))
        def _():
            start_chunk(b + 1, ahead - n_chunks, sl_ahead)

        wait_chunk(b, c, sl)
        s = jnp.zeros((N_HEADS, CHUNK_TOKENS), F32)
        for g in range(N_KV_HEADS):
            s_g = jnp.dot(qb, buf_ref[sl, 0, g].astype(BF16), preferred_element_type=F32)
            s = jnp.where(head_grp == g, s_g, s)
        s = s + bias_ref[0, :, _chunk_cols(c)]
        m_new = jnp.maximum(m_run, jnp.max(s, axis=1, keepdims=True))
        p = jnp.exp2(s - m_new).astype(BF16)
        scale = jnp.exp2(m_run - m_new)
        l_new = scale * l_run + jnp.sum(p.astype(F32), axis=1, keepdims=True)
        pv = jnp.zeros((N_HEADS, HEAD_DIM), F32)
        for g in range(N_KV_HEADS):
            o_g = lax.dot_general(p, buf_ref[sl, 1, g].astype(BF16), _NT, preferred_element_type=F32)
            pv = jnp.where(head_grp == g, o_g, pv)
        return m_new, l_new, scale * acc + pv

    s_self = jnp.sum(qb.astype(F32) * knew_ref[0], axis=1, keepdims=True)
    m0 = jnp.where(sel_self, s_self, NEG_BIG)
    l0 = jnp.where(sel_self, 1.0, 0.0) + jnp.zeros_like(s_self)
    _, l_fin, acc = lax.fori_loop(0, n_chunks, step, (m0, l0, l0 * vnew_ref[0]))
    o_ref[0] = acc / l_fin


def _dsa_sample(q, qi, kiw, k_new, v_new, cache_k, cache_v, cache_kidx, page_table):
    bs = q.shape[0]
    n_pages = page_table.shape[1]
    assert n_pages % PAGES_PER_CHUNK == 0 and n_pages // PAGES_PER_CHUNK >= KV_SLOTS - 1
    n_chunks = n_pages // PAGES_PER_CHUNK
    n_keys = n_pages * PAGE_SIZE
    topk = min(TOPK_MAX, (n_keys + 1) // 4)

    kidx_t = jnp.transpose(cache_kidx, (0, 2, 1))
    k_t = jnp.transpose(cache_k, (0, 2, 3, 1))
    v_t = jnp.transpose(cache_v, (0, 2, 3, 1))
    qi_h = qi.reshape(bs, IDX_HEADS, IDX_DIM)
    w_idx = (kiw[:, HEAD_DIM:HEAD_DIM + IDX_HEADS] * (IDX_DIM ** -0.5 * IDX_HEADS ** -0.5))[:, :, None]
    ki_new = jnp.broadcast_to(kiw[:, None, :IDX_DIM], (bs, IDX_HEADS, IDX_DIM))
    q_h = q.reshape(bs, N_HEADS, HEAD_DIM)
    k_rep = jnp.repeat(k_new.reshape(bs, N_KV_HEADS, HEAD_DIM), GQA_GROUP, axis=1)
    v_rep = jnp.repeat(v_new.reshape(bs, N_KV_HEADS, HEAD_DIM), GQA_GROUP, axis=1)

    per_b = lambda shape: pl.BlockSpec((1,) + shape, lambda bi, pt: (bi, 0, 0))
    any_spec = pl.BlockSpec(memory_space=pl.ANY)
    keys, key_self = pl.pallas_call(
        functools.partial(_idx_sample_kernel, n_pages=n_pages),
        grid_spec=pltpu.PrefetchScalarGridSpec(
            num_scalar_prefetch=1, grid=(bs,),
            in_specs=[per_b((IDX_HEADS, IDX_DIM)), per_b((IDX_HEADS, 1)), per_b((IDX_HEADS, IDX_DIM)),
                      any_spec],
            out_specs=[per_b((n_chunks, CHUNK_TOKENS)), per_b((1, LANES))],
            scratch_shapes=[pltpu.VMEM((2, IDX_DIM, n_keys), F32), pltpu.SemaphoreType.DMA((2,))]),
        out_shape=[jax.ShapeDtypeStruct((bs, n_chunks, CHUNK_TOKENS), jnp.int32),
                   jax.ShapeDtypeStruct((bs, 1, LANES), jnp.int32)],
        compiler_params=_params(1),
    )(page_table, qi_h, w_idx, ki_new, kidx_t)

    bias, sel_self = pl.pallas_call(
        functools.partial(_select_sample_kernel, topk=topk),
        out_shape=[jax.ShapeDtypeStruct((bs, n_keys), F32), jax.ShapeDtypeStruct((bs, LANES), F32)],
        compiler_params=pltpu.CompilerParams(vmem_limit_bytes=VMEM_LIMIT_BYTES),
    )(keys.reshape(bs, n_keys), key_self.reshape(bs, LANES))

    out = pl.pallas_call(
        functools.partial(_attn_sample_kernel, n_pages=n_pages),
        grid_spec=pltpu.PrefetchScalarGridSpec(
            num_scalar_prefetch=1, grid=(bs,),
            in_specs=[per_b((N_HEADS, HEAD_DIM)), per_b((N_HEADS, HEAD_DIM)), per_b((N_HEADS, HEAD_DIM)),
                      per_b((1, n_keys)), per_b((1, LANES)), any_spec, any_spec],
            out_specs=per_b((N_HEADS, HEAD_DIM)),
            scratch_shapes=[pltpu.VMEM((KV_SLOTS, 2, N_KV_HEADS, HEAD_DIM, CHUNK_TOKENS), F32),
                            pltpu.SemaphoreType.DMA((KV_SLOTS,))]),
        out_shape=jax.ShapeDtypeStruct((bs, N_HEADS, HEAD_DIM), F32),
        compiler_params=_params(1),
    )(page_table, q_h, k_rep, v_rep, bias.reshape(bs, 1, n_keys), sel_self.reshape(bs, 1, LANES), k_t, v_t)
    return out.reshape(bs, Q_COLS)


def _matmul_ln_kernel(a_ref, w_ref, x_ref, g_ref, b_ref, o_ref, *, alpha):
    mix = jnp.dot(a_ref[...].astype(BF16), w_ref[...], preferred_element_type=F32)
    o_ref[...] = _layernorm(alpha * x_ref[...] + mix, g_ref[...], b_ref[...])


def _matmul_ln(a, w, x, g, b, *, alpha):
    m, d = x.shape
    return pl.pallas_call(
        functools.partial(_matmul_ln_kernel, alpha=alpha),
        out_shape=jax.ShapeDtypeStruct((m, d), F32),
        compiler_params=pltpu.CompilerParams(vmem_limit_bytes=VMEM_LIMIT_BYTES),
    )(a, w, x, g.reshape(1, d), b.reshape(1, d))


HALO = 16


def _pool_mix(diffs, wp_ref, scale):
    outs = [jnp.dot(d.astype(BF16), wp_ref[g], preferred_element_type=F32) for g, d in enumerate(diffs)]
    return jnp.concatenate(outs, axis=1) * scale


def _pool_prompt_kernel(x_ref, halo_ref, wp_ref, sc_ref, g_ref, b_ref, o_ref, ext_ref,
                        *, tm, tiles_per_seq, alpha):
    i = pl.program_id(0)
    t_in_seq = i % tiles_per_seq
    x = x_ref[...]
    ext_ref[HALO:, :] = x
    ext_ref[:HALO, :] = jnp.where(t_in_seq == 0, 0.0, halo_ref[...])
    pos = t_in_seq * tm + lax.broadcasted_iota(jnp.int32, (tm, 1), 0)
    pool_ch = x.shape[1] // len(POOL_WINDOWS)
    diffs = []
    for g, w in enumerate(POOL_WINDOWS):
        ch = slice(g * pool_ch, (g + 1) * pool_ch)
        tot = x[:, ch]
        for s in range(1, w):
            tot = tot + ext_ref[HALO - s:HALO - s + tm, ch]
        cnt = jnp.minimum(pos + 1, w).astype(F32)
        diffs.append(tot / cnt - x[:, ch])
    mix = _pool_mix(diffs, wp_ref, sc_ref[...])
    o_ref[...] = _layernorm(alpha * x + mix, g_ref[...], b_ref[...])


def _pool_prompt(x, wp, scale, g, b, *, seq, tm, alpha):
    m, d = x.shape
    assert seq % tm == 0 and tm % HALO == 0
    tiles_per_seq = seq // tm
    halo_blocks = tm // HALO
    return pl.pallas_call(
        functools.partial(_pool_prompt_kernel, tm=tm, tiles_per_seq=tiles_per_seq, alpha=alpha),
        grid=(m // tm,),
        in_specs=[pl.BlockSpec((tm, d), lambda i: (i, 0)),
                  pl.BlockSpec((HALO, d), lambda i: (jnp.maximum(i * halo_blocks - 1, 0), 0)),
                  _const_spec(wp.shape), _const_spec((1, d)), _const_spec((1, d)), _const_spec((1, d))],
        out_specs=pl.BlockSpec((tm, d), lambda i: (i, 0)),
        out_shape=jax.ShapeDtypeStruct((m, d), F32),
        scratch_shapes=[pltpu.VMEM((tm + HALO, d), F32)],
        compiler_params=_params(1),
    )(x, x, wp, scale.reshape(1, d), g.reshape(1, d), b.reshape(1, d))


def _pool_sample_kernel(ext_ref, wp_ref, sc_ref, g_ref, b_ref, o_ref, *, alpha):
    n_rows = ext_ref.shape[1]
    x = ext_ref[:, n_rows - 1, :]
    pool_ch = x.shape[1] // len(POOL_WINDOWS)
    diffs = []
    for g, w in enumerate(POOL_WINDOWS):
        ch = slice(g * pool_ch, (g + 1) * pool_ch)
        tot = x[:, ch]
        for s in range(1, w):
            tot = tot + ext_ref[:, n_rows - 1 - s, ch]
        diffs.append(tot / float(w) - x[:, ch])
    mix = _pool_mix(diffs, wp_ref, sc_ref[...])
    o_ref[...] = _layernorm(alpha * x + mix, g_ref[...], b_ref[...])


def _pool_sample(ext, wp, scale, g, b, *, alpha):
    bs, n_rows, d = ext.shape
    assert n_rows >= max(POOL_WINDOWS)
    return pl.pallas_call(
        functools.partial(_pool_sample_kernel, alpha=alpha),
        out_shape=jax.ShapeDtypeStruct((bs, d), F32),
        compiler_params=pltpu.CompilerParams(vmem_limit_bytes=VMEM_LIMIT_BYTES),
    )(ext, wp, scale.reshape(1, d), g.reshape(1, d), b.reshape(1, d))


def _row_tile(m, target):
    return target if m % target == 0 else m


def kernel(x_prompt, x_sample, cache_k, cache_v, cache_kidx, state_pool, page_table, ln_g, ln_b,
           ffn1_wi, ffn1_wo, ffn2_wi, ffn2_wo, attn_w_in, attn_w_o, pool_w, pool_scale):
    batch, seq, d = x_prompt.shape
    bs, dec_seq, _ = x_sample.shape
    assert dec_seq == 1
    depth = ln_g.shape[0]
    alpha = (2 * depth) ** 0.25
    past = page_table.shape[1] * PAGE_SIZE

    xp = x_prompt.reshape(batch * seq, d)
    xs = x_sample.reshape(bs, d)
    tm_p = _row_tile(batch * seq, 512)
    tq = _row_tile(seq, 256)

    def ffn(x, wi, wo, g, b, tm):
        return _ffn_ln(x, wi.astype(BF16), wo.astype(BF16), g, b, tm=tm, alpha=alpha)

    tabs_p = _rope_tables(jnp.arange(seq, dtype=jnp.int32).astype(F32))
    tabs_s = _rope_tables((past + jnp.zeros((bs,), jnp.int32)).astype(F32))

    kp_l, vp_l, kip_l, poolp_l = [], [], [], []
    ks_l, vs_l, kis_l, pools_l = [], [], [], []
    for l in range(depth):
        j = l // N_MIXERS
        xp = ffn(xp, ffn1_wi[l], ffn1_wo[l], ln_g[l, 0], ln_b[l, 0], tm_p)
        xs = ffn(xs, ffn1_wi[l], ffn1_wo[l], ln_g[l, 0], ln_b[l, 0], bs)
        if l % N_MIXERS == 0:
            w_in = attn_w_in[j].astype(BF16)
            w_in = jnp.pad(w_in, ((0, 0), (0, W_IN_PAD - w_in.shape[1])))
            w_o = attn_w_o[j].astype(BF16)
            q, k_t, v_t, qi, kiw_t, kab, vab, kiab = _attn_proj(xp, w_in, tabs_p, tm=_row_tile(seq, 512),
                                                                attn_layouts=True)
            xp = _dsa_prompt(q, qi, kiw_t, kiab, kab, vab, xp, w_o, ln_g[l, 1], ln_b[l, 1],
                             batch=batch, seq=seq, tq=tq, alpha=alpha)
            kp_l.append(jnp.transpose(k_t, (0, 3, 1, 2)))
            vp_l.append(jnp.transpose(v_t, (0, 3, 1, 2)))
            kip_l.append(jnp.transpose(kiw_t[:, :IDX_DIM, :], (0, 2, 1)))

            qs, k_s, v_s, qis, kiws = _attn_proj(xs, w_in, tabs_s, tm=bs, attn_layouts=False)
            attn_s = _dsa_sample(qs, qis, kiws, k_s, v_s, cache_k[j], cache_v[j], cache_kidx[j],
                                 page_table)
            xs = _matmul_ln(attn_s, w_o, xs, ln_g[l, 1], ln_b[l, 1], alpha=alpha)
            ks_l.append(k_s.reshape(bs, 1, N_KV_HEADS, HEAD_DIM))
            vs_l.append(v_s.reshape(bs, 1, N_KV_HEADS, HEAD_DIM))
            kis_l.append(kiws[:, :IDX_DIM].reshape(bs, 1, IDX_DIM))
        else:
            wp = pool_w[j].astype(BF16)
            poolp_l.append(xp.reshape(batch, seq, d)[:, seq - POOL_BUF:])
            ext = jnp.concatenate([state_pool[j], xs[:, None, :]], axis=1)
            pools_l.append(ext[:, ext.shape[1] - POOL_BUF:])
            xp = _pool_prompt(xp, wp, pool_scale[j], ln_g[l, 1], ln_b[l, 1],
                              seq=seq, tm=_row_tile(seq, 512), alpha=alpha)
            xs = _pool_sample(ext, wp, pool_scale[j], ln_g[l, 1], ln_b[l, 1], alpha=alpha)
        xp = ffn(xp, ffn2_wi[l], ffn2_wo[l], ln_g[l, 2], ln_b[l, 2], tm_p)
        xs = ffn(xs, ffn2_wi[l], ffn2_wo[l], ln_g[l, 2], ln_b[l, 2], bs)

    return (xp.reshape(batch, seq, d), xs.reshape(bs, 1, d),
            jnp.stack(kp_l), jnp.stack(vp_l), jnp.stack(kip_l), jnp.stack(poolp_l),
            jnp.stack(ks_l), jnp.stack(vs_l), jnp.stack(kis_l), jnp.stack(pools_l))
```

```python
import functools

import jax
import jax.numpy as jnp
from jax import lax
from jax.experimental import pallas as pl
from jax.experimental.pallas import tpu as pltpu

N_MIXERS = 2
N_HEADS = 16
HEAD_DIM = 64
N_KV_HEADS = 4
GQA_GROUP = N_HEADS // N_KV_HEADS
ROT_DIM = HEAD_DIM // 4
ROPE_THETA = 500000.0
IDX_HEADS = 8
IDX_DIM = 64
TOPK_MAX = 256
PAGE_SIZE = 128
POOL_WINDOWS = (2, 4, 8, 16)
POOL_BUF = max(POOL_WINDOWS) - 1
LN_EPS = 1e-5
Q_COLS = N_HEADS * HEAD_DIM
KV_COLS = N_KV_HEADS * HEAD_DIM
QI_COLS = IDX_HEADS * IDX_DIM

LANES = 128
SUBLANES = 8
BF16_ROWS = 16
VMEM_LIMIT_BYTES = 56 * 1024 * 1024

F32 = jnp.float32
BF16 = jnp.bfloat16
INT_MIN = -(2 ** 31)
NEG_BIG = -1e30

_NT = (((1,), (1,)), ((), ()))


def _layernorm(y, g, b):
    mu = jnp.mean(y, axis=-1, keepdims=True)
    yc = y - mu
    var = jnp.mean(yc * yc, axis=-1, keepdims=True)
    return yc * lax.rsqrt(var + LN_EPS) * g + b


def _const_spec(shape):
    nd = len(shape)
    return pl.BlockSpec(shape, lambda *_: (0,) * nd, pipeline_mode=pl.Buffered(1))


def _params(n_grid):
    return pltpu.CompilerParams(dimension_semantics=("arbitrary",) * n_grid,
                                vmem_limit_bytes=VMEM_LIMIT_BYTES)


FFN_CHUNK = 256


def _ffn_kernel(x_ref, wi_ref, wo_ref, g_ref, b_ref, o_ref, acc_ref, *, d_ff, alpha):
    x = x_ref[...]
    xb = x.astype(BF16)
    for c in range(d_ff // FFN_CHUNK):
        lo = c * FFN_CHUNK
        hg = jnp.dot(xb, wi_ref[:, lo:lo + FFN_CHUNK], preferred_element_type=F32)
        hu = jnp.dot(xb, wi_ref[:, d_ff + lo:d_ff + lo + FFN_CHUNK], preferred_element_type=F32)
        a = (hg * jax.nn.sigmoid(hg) * hu).astype(BF16)
        part = jnp.dot(a, wo_ref[lo:lo + FFN_CHUNK, :], preferred_element_type=F32)
        if c == 0:
            acc_ref[...] = part
        else:
            acc_ref[...] += part
    o_ref[...] = _layernorm(alpha * x + 0.5 * acc_ref[...], g_ref[...], b_ref[...])


def _ffn_ln(x, wi, wo, g, b, *, tm, alpha):
    m, d = x.shape
    d_ff = wo.shape[0]
    assert m % tm == 0 and d_ff % FFN_CHUNK == 0
    return pl.pallas_call(
        functools.partial(_ffn_kernel, d_ff=d_ff, alpha=alpha),
        grid=(m // tm,),
        in_specs=[pl.BlockSpec((tm, d), lambda i: (i, 0)),
                  _const_spec(wi.shape), _const_spec(wo.shape),
                  _const_spec((1, d)), _const_spec((1, d))],
        out_specs=pl.BlockSpec((tm, d), lambda i: (i, 0)),
        out_shape=jax.ShapeDtypeStruct((m, d), F32),
        scratch_shapes=[pltpu.VMEM((tm, d), F32)],
        compiler_params=_params(1),
    )(x, wi, wo, g.reshape(1, d), b.reshape(1, d))


O_K = Q_COLS
O_V = O_K + KV_COLS
O_QI = O_V + KV_COLS
O_KI = O_QI + QI_COLS
W_IN_PAD = O_KI + LANES
LOG2E = 1.4426950408889634
Q_SCALE = HEAD_DIM ** -0.5 * LOG2E


def _proj_kernel(x_ref, w_ref, cos_ref, sa_ref, sb_ref, q_ref, k_ref, v_ref, qi_ref, kiw_ref,
                 *attn_refs):
    xb = x_ref[...].astype(BF16)
    cos = cos_ref[...]
    sin_a = sa_ref[...]
    sin_b = sb_ref[...]
    tm = xb.shape[0]
    low = lax.broadcasted_iota(jnp.int32, (tm, LANES), 1) < HEAD_DIM

    def rope(t):
        return (t * cos + pltpu.roll(t, ROT_DIM // 2, 1) * sin_a
                + pltpu.roll(t, LANES - ROT_DIM // 2, 1) * sin_b)

    def proj(lo, width):
        return jnp.dot(xb, w_ref[:, lo:lo + width], preferred_element_type=F32)

    pq = proj(0, Q_COLS)
    for t in range(Q_COLS // LANES):
        q_ref[:, t * LANES:(t + 1) * LANES] = (
            rope(pq[:, t * LANES:(t + 1) * LANES]) * Q_SCALE).astype(BF16)
    pqi = proj(O_QI, QI_COLS)
    for t in range(QI_COLS // LANES):
        qi_ref[:, t * LANES:(t + 1) * LANES] = rope(pqi[:, t * LANES:(t + 1) * LANES]).astype(BF16)
    pkw = proj(O_KI, LANES)
    kw = jnp.where(low, rope(pkw), pkw)
    if attn_refs:
        kiw_ref[0] = kw.T
    else:
        kiw_ref[...] = kw

    pk = proj(O_K, KV_COLS)
    pv = proj(O_V, KV_COLS)
    if attn_refs:
        kab_ref, vab_ref, kiab_ref = attn_refs
        kiab_ref[0] = jnp.where(low, kw, 0.0).astype(BF16)
        kiab_ref[1] = jnp.where(low, 0.0, pltpu.roll(kw, HEAD_DIM, 1)).astype(BF16)
        lane = lax.broadcasted_iota(jnp.int32, (tm, LANES), 1)
        ones_a = jnp.where(lane == HEAD_DIM, 1.0, 0.0)
        ones_b = jnp.where(lane == 0, 1.0, 0.0)
    for t in range(KV_COLS // LANES):
        kr = rope(pk[:, t * LANES:(t + 1) * LANES])
        vv = pv[:, t * LANES:(t + 1) * LANES]
        if not attn_refs:
            k_ref[:, t * LANES:(t + 1) * LANES] = kr
            v_ref[:, t * LANES:(t + 1) * LANES] = vv
        else:
            g0, g1 = 2 * t, 2 * t + 1
            k_t, v_t = kr.T, vv.T
            for half, g in enumerate((g0, g1)):
                k_ref[0, g] = k_t[half * HEAD_DIM:(half + 1) * HEAD_DIM]
                v_ref[0, g] = v_t[half * HEAD_DIM:(half + 1) * HEAD_DIM]
            krr = pltpu.roll(kr, HEAD_DIM, 1)
            kab_ref[2 * g0] = jnp.where(low, kr, 0.0).astype(BF16)
            kab_ref[2 * g0 + 1] = jnp.where(low, 0.0, krr).astype(BF16)
            kab_ref[2 * g1] = jnp.where(low, krr, 0.0).astype(BF16)
            kab_ref[2 * g1 + 1] = jnp.where(low, 0.0, kr).astype(BF16)
            vvr = pltpu.roll(vv, HEAD_DIM, 1)
            vab_ref[2 * g0] = jnp.where(low, vv, ones_a).astype(BF16)
            vab_ref[2 * g0 + 1] = jnp.where(low, ones_b, vvr).astype(BF16)
            vab_ref[2 * g1] = jnp.where(low, vvr, ones_a).astype(BF16)
            vab_ref[2 * g1 + 1] = jnp.where(low, ones_b, vv).astype(BF16)


def _rope_tables(pos):
    half = ROT_DIM // 2
    freqs = ROPE_THETA ** (-jnp.arange(half, dtype=F32) / half)
    ang = pos[:, None] * freqs[None, :]
    cos, sin = jnp.cos(ang), jnp.sin(ang)
    t = pos.shape[0]
    rest = HEAD_DIM - ROT_DIM
    zeros_h = jnp.zeros((t, half), F32)
    c64 = jnp.concatenate([cos, cos, jnp.ones((t, rest), F32)], axis=1)
    a64 = jnp.concatenate([zeros_h, sin, jnp.zeros((t, rest), F32)], axis=1)
    b64 = jnp.concatenate([-sin, zeros_h, jnp.zeros((t, rest), F32)], axis=1)
    rep = LANES // HEAD_DIM
    return tuple(jnp.tile(a, (1, rep)) for a in (c64, a64, b64))


def _attn_proj(x, w_pad, tables, *, tm, attn_layouts):
    m, d = x.shape
    t_rows = tables[0].shape[0]
    assert m % tm == 0 and t_rows % tm == 0
    n_t = t_rows // tm
    row = lambda i: (i, 0)
    tab_spec = pl.BlockSpec((tm, LANES), lambda i: (i % n_t, 0))
    n_kv2 = 2 * N_KV_HEADS
    if attn_layouts:
        n_seq = m // t_rows
        kv_spec = pl.BlockSpec((1, N_KV_HEADS, HEAD_DIM, tm), lambda i: (i // n_t, 0, 0, i % n_t))
        kv_shape = jax.ShapeDtypeStruct((n_seq, N_KV_HEADS, HEAD_DIM, t_rows), F32)
        kiw_spec = pl.BlockSpec((1, LANES, tm), lambda i: (i // n_t, 0, i % n_t))
        kiw_shape = jax.ShapeDtypeStruct((n_seq, LANES, t_rows), F32)
    else:
        kv_spec = pl.BlockSpec((tm, KV_COLS), row)
        kv_shape = jax.ShapeDtypeStruct((m, KV_COLS), F32)
        kiw_spec = pl.BlockSpec((tm, LANES), row)
        kiw_shape = jax.ShapeDtypeStruct((m, LANES), F32)
    out_specs = [pl.BlockSpec((tm, Q_COLS), row), kv_spec, kv_spec, pl.BlockSpec((tm, QI_COLS), row),
                 kiw_spec]
    out_shape = [jax.ShapeDtypeStruct((m, Q_COLS), BF16), kv_shape, kv_shape,
                 jax.ShapeDtypeStruct((m, QI_COLS), BF16), kiw_shape]
    if attn_layouts:
        out_specs += [pl.BlockSpec((n_kv2, tm, LANES), lambda i: (0, i, 0)),
                      pl.BlockSpec((n_kv2, tm, LANES), lambda i: (0, i, 0)),
                      pl.BlockSpec((2, tm, LANES), lambda i: (0, i, 0))]
        out_shape += [jax.ShapeDtypeStruct((n_kv2, m, LANES), BF16),
                      jax.ShapeDtypeStruct((n_kv2, m, LANES), BF16),
                      jax.ShapeDtypeStruct((2, m, LANES), BF16)]
    return pl.pallas_call(
        _proj_kernel,
        grid=(m // tm,),
        in_specs=[pl.BlockSpec((tm, d), row), _const_spec(w_pad.shape), tab_spec, tab_spec, tab_spec],
        out_specs=out_specs,
        out_shape=out_shape,
        compiler_params=_params(1),
    )(x, w_pad, *tables)


def _order_key(score):
    score = jnp.where(score == 0.0, 0.0, score)
    bits = pltpu.bitcast(score, jnp.int32)
    return bits ^ ((bits >> 31) & 0x7FFFFFFF)


def _kth_largest(count_ge, shape, n_total, k):
    def body(it, state):
        t, n_t = state
        cand = t + lax.shift_left(jnp.int32(1), 31 - it)
        n_c = count_ge(cand)
        take = n_c >= k
        return jnp.where(take, cand, t), jnp.where(take, n_c, n_t)
    t0 = jnp.full(shape, INT_MIN, jnp.int32)
    return lax.fori_loop(0, 32, body, (t0, jnp.zeros_like(t0) + n_total))


def _dsa_prompt_kernel(q_ref, qi_ref, kiwq_ref, kiab_ref, kab_ref, vab_ref, x_ref, wo_ref,
                       g_ref, b_ref, o_ref, key_ref, acc_ref, m_ref, *, tq, topk, alpha):
    i = pl.program_id(1)
    n_chunks = i + 1
    n_pairs = (n_chunks + 1) // 2
    seq = key_ref.shape[0] - tq
    kpos = lax.broadcasted_iota(jnp.int32, (tq, tq), 0)
    qpos = lax.broadcasted_iota(jnp.int32, (tq, tq), 1)
    kpos2 = lax.broadcasted_iota(jnp.int32, (2 * tq, tq), 0)

    def chunk_off(j):
        return pl.multiple_of(j * tq, tq)

    w_idx = kiwq_ref[0, HEAD_DIM:HEAD_DIM + IDX_HEADS, :] * (IDX_DIM ** -0.5 * IDX_HEADS ** -0.5)
    key_ref[pl.ds(chunk_off(n_chunks), tq), :] = jnp.full((tq, tq), INT_MIN, jnp.int32)

    def score_chunk(j, carry):
        off = chunk_off(j)
        sc = jnp.zeros((tq, tq), F32)
        for h in range(IDX_HEADS):
            qt = qi_ref[:, (h // 2) * LANES:(h // 2 + 1) * LANES]
            s = lax.dot_general(kiab_ref[h % 2, pl.ds(off, tq), :], qt, _NT, preferred_element_type=F32)
            sc = sc + jnp.maximum(s, 0.0) * w_idx[h:h + 1]
        key = _order_key(sc)
        key = jnp.where(jnp.logical_and(j == i, kpos > qpos), INT_MIN, key)
        key_ref[pl.ds(off, tq), :] = key
        return carry

    def chunks_in_groups(chunk_fn, group):
        def step(js, carry):
            for u in range(group):
                chunk_fn(group * js + u, carry)
            return carry
        lax.fori_loop(0, n_chunks // group, step, 0)
        done = (n_chunks // group) * group
        size = group // 2
        while size >= 1:
            @pl.when((n_chunks & size) != 0)
            def _(done=done, size=size):
                for u in range(size):
                    chunk_fn(done + u, 0)
            done = done + (n_chunks & size)
            size //= 2

    chunks_in_groups(score_chunk, 4)

    acc_rows = 4 * SUBLANES

    def fold_rows(x):
        return jnp.sum(x.reshape(2 * tq // acc_rows, acc_rows, tq), axis=0)

    def count_keys(pred):
        def body(jp, cnt):
            off = pl.multiple_of(jp * 2 * tq, 2 * tq)
            return cnt + fold_rows(jnp.where(pred(key_ref[pl.ds(off, 2 * tq), :], off), 1, 0))
        cnt = lax.fori_loop(0, n_pairs, body, jnp.zeros((acc_rows, tq), jnp.int32))
        return jnp.sum(cnt, axis=0, keepdims=True)

    thr, n_ge = _kth_largest(lambda cand: count_keys(lambda x, off: x >= cand), (1, tq),
                             n_pairs * 2 * tq, topk)
    tie = jnp.logical_and(n_ge > topk, thr > INT_MIN)

    @pl.when(jnp.max(jnp.where(tie, 1, 0)) > 0)
    def _():
        need = topk - count_keys(lambda x, off: x > thr)
        n_bits = (seq - 1).bit_length()

        def pbody(it, p):
            cand = p + lax.shift_left(jnp.int32(1), n_bits - 1 - it)
            c = count_keys(lambda x, off: jnp.logical_and(x == thr, (kpos2 + off) < cand))
            return jnp.where(c < need, cand, p)
        last = lax.fori_loop(0, n_bits, pbody, jnp.zeros((1, tq), jnp.int32))

        def patch(j, carry):
            off = chunk_off(j)
            x = key_ref[pl.ds(off, tq), :]
            drop = jnp.logical_and(x == thr, (kpos + off) > last)
            key_ref[pl.ds(off, tq), :] = jnp.where(jnp.logical_and(tie, drop), thr - 1, x)
            return carry
        lax.fori_loop(0, n_chunks, patch, 0)

    thr = jnp.maximum(thr, INT_MIN + 1)

    acc_ref[...] = jnp.zeros_like(acc_ref)
    m_ref[...] = jnp.full_like(m_ref, NEG_BIG)
    lane_tiles = tq // LANES

    def attn_chunk(j, carry):
        off = chunk_off(j)
        bias = jnp.where(key_ref[pl.ds(off, tq), :] >= thr, 0.0, NEG_BIG).T
        for h in range(N_HEADS):
            kv = 2 * (h // GQA_GROUP) + h % 2
            qt = q_ref[:, (h // 2) * LANES:(h // 2 + 1) * LANES]
            s = lax.dot_general(qt, kab_ref[kv, pl.ds(off, tq), :], _NT,
                                preferred_element_type=F32) + bias
            m_prev = m_ref[h]
            m_next = jnp.maximum(m_prev, jnp.max(s, axis=1, keepdims=True))
            p = jnp.exp2(s - jnp.concatenate([m_next] * lane_tiles, axis=1))
            acc_ref[h] = (jnp.exp2(m_prev - m_next) * acc_ref[h]
                          + jnp.dot(p.astype(BF16), vab_ref[kv, pl.ds(off, tq), :],
                                    preferred_element_type=F32))
            m_ref[h] = m_next
        return carry

    chunks_in_groups(attn_chunk, 4)

    low = lax.broadcasted_iota(jnp.int32, (tq, LANES), 1) < HEAD_DIM
    tiles = []
    for t in range(N_HEADS // 2):
        a0 = acc_ref[2 * t]
        a1 = acc_ref[2 * t + 1]
        o0 = a0 / a0[:, HEAD_DIM:HEAD_DIM + 1]
        o1 = a1 / a1[:, 0:1]
        tiles.append(jnp.where(low, o0, o1).astype(BF16))
    attn = jnp.concatenate(tiles, axis=1)
    mix = jnp.dot(attn, wo_ref[...], preferred_element_type=F32)
    o_ref[...] = _layernorm(alpha * x_ref[...] + mix, g_ref[...], b_ref[...])


def _dsa_prompt(q, qi, kiw, kiab, kab, vab, x, wo, g, b, *, batch, seq, tq, alpha):
    m, d = x.shape
    assert seq % tq == 0 and tq % LANES == 0
    nq = seq // tq
    topk = min(TOPK_MAX, seq // 4)
    row = lambda bi, i: (bi * nq + i, 0)
    per_seq = lambda bi, i: (0, bi, 0)
    seq_spec = lambda n: pl.BlockSpec((n, seq, LANES), per_seq, pipeline_mode=pl.Buffered(1))
    return pl.pallas_call(
        functools.partial(_dsa_prompt_kernel, tq=tq, topk=topk, alpha=alpha),
        grid=(batch, nq),
        in_specs=[pl.BlockSpec((tq, Q_COLS), row),
                  pl.BlockSpec((tq, QI_COLS), row),
                  pl.BlockSpec((1, LANES, tq), lambda bi, i: (bi, 0, i)),
                  seq_spec(2), seq_spec(2 * N_KV_HEADS), seq_spec(2 * N_KV_HEADS),
                  pl.BlockSpec((tq, d), row),
                  _const_spec(wo.shape), _const_spec((1, d)), _const_spec((1, d))],
        out_specs=pl.BlockSpec((tq, d), row),
        out_shape=jax.ShapeDtypeStruct((m, d), F32),
        scratch_shapes=[pltpu.VMEM((seq + tq, tq), jnp.int32),
                        pltpu.VMEM((N_HEADS, tq, LANES), F32),
                        pltpu.VMEM((N_HEADS, tq, LANES), F32)],
        compiler_params=_params(2),
    )(q, qi, kiw, kiab, kab, vab, x, wo, g.reshape(1, d), b.reshape(1, d))


PAGES_PER_CHUNK = 16
KV_SLOTS = 3
CHUNK_TOKENS = PAGES_PER_CHUNK * PAGE_SIZE


def _chunk_cols(c):
    return pl.ds(pl.multiple_of(c * CHUNK_TOKENS, CHUNK_TOKENS), CHUNK_TOKENS)


def _idx_sample_kernel(pt_ref, qi_ref, widx_ref, kinew_ref, kidx_hbm, key_ref, keyself_ref,
                       kib_ref, sem, *, n_pages):
    b = pl.program_id(0)
    n_b = pl.num_programs(0)
    slot = b % 2
    n_chunks = n_pages // PAGES_PER_CHUNK

    def page_copy(sample, sl, pg):
        return pltpu.make_async_copy(kidx_hbm.at[pt_ref[sample, pg]],
                                     kib_ref.at[sl, :, pl.ds(pg * PAGE_SIZE, PAGE_SIZE)], sem.at[sl])

    def start_sample(sample, sl):
        for pg in range(n_pages):
            page_copy(sample, sl, pg).start()

    def wait_sample(sample, sl):
        for pg in range(n_pages):
            page_copy(sample, sl, pg).wait()

    @pl.when(b == 0)
    def _():
        start_sample(0, 0)

    @pl.when(b + 1 < n_b)
    def _():
        start_sample(b + 1, 1 - slot)

    wait_sample(b, slot)
    w_idx = widx_ref[0]
    qi = qi_ref[0]

    def idx_chunk(c, carry):
        s_idx = jnp.dot(qi, kib_ref[slot, :, _chunk_cols(c)].astype(BF16), preferred_element_type=F32)
        sc = jnp.sum(jnp.maximum(s_idx, 0.0) * w_idx, axis=0, keepdims=True)
        key_ref[0, pl.ds(c, 1), :] = _order_key(sc)
        return carry
    lax.fori_loop(0, n_chunks, idx_chunk, 0)

    s_self = jnp.sum(qi.astype(F32) * kinew_ref[0], axis=1, keepdims=True)
    sc_self = jnp.sum(jnp.maximum(s_self, 0.0) * w_idx, axis=0, keepdims=True)
    keyself_ref[0] = jnp.broadcast_to(_order_key(sc_self), (1, LANES))


def _select_sample_kernel(key_ref, keyself_ref, bias_ref, selself_ref, *, topk):
    bs, n_keys = key_ref.shape
    key_self = keyself_ref[:, 0:1]
    pos = lax.broadcasted_iota(jnp.int32, (bs, n_keys), 1)
    pos_self = n_keys

    def count(pred_past, pred_self):
        c = jnp.sum(jnp.where(pred_past(key_ref[...]), 1, 0), axis=1, keepdims=True)
        return c + jnp.where(pred_self, 1, 0)

    thr, n_ge = _kth_largest(lambda cand: count(lambda k: k >= cand, key_self >= cand), (bs, 1),
                             n_keys + 1, topk)
    n_gt = count(lambda k: k > thr, key_self > thr)
    need = topk - n_gt
    n_bits = pos_self.bit_length()

    def pbody(it, p):
        cand = p + lax.shift_left(jnp.int32(1), n_bits - 1 - it)
        c = count(lambda k: jnp.logical_and(k == thr, pos < cand),
                  jnp.logical_and(key_self == thr, pos_self < cand))
        return jnp.where(c < need, cand, p)
    last = lax.fori_loop(0, n_bits, pbody, jnp.zeros((bs, 1), jnp.int32))
    last = jnp.where(n_ge > topk, last, pos_self)
    key = key_ref[...]
    sel = jnp.logical_or(key > thr, jnp.logical_and(key == thr, pos <= last))
    sel_self = jnp.logical_or(key_self > thr, jnp.logical_and(key_self == thr, pos_self <= last))
    bias_ref[...] = jnp.where(sel, 0.0, NEG_BIG)
    selself_ref[...] = jnp.broadcast_to(jnp.where(sel_self, 1.0, 0.0), selself_ref.shape)


def _attn_sample_kernel(pt_ref, q_ref, knew_ref, vnew_ref, bias_ref, selself_ref, k_hbm, v_hbm, o_ref,
                        buf_ref, sem, *, n_pages):
    b = pl.program_id(0)
    n_b = pl.num_programs(0)
    n_chunks = n_pages // PAGES_PER_CHUNK
    first = b * n_chunks

    def chunk_copies(sample, c, sl):
        cps = []
        for pg in range(PAGES_PER_CHUNK):
            page = pt_ref[sample, c * PAGES_PER_CHUNK + pg]
            cols = pl.ds(pg * PAGE_SIZE, PAGE_SIZE)
            for which, src in enumerate((k_hbm, v_hbm)):
                cps.append(pltpu.make_async_copy(src.at[page], buf_ref.at[sl, which, :, :, cols], sem.at[sl]))
        return cps

    def start_chunk(sample, c, sl):
        for cp in chunk_copies(sample, c, sl):
            cp.start()

    def wait_chunk(sample, c, sl):
        for cp in chunk_copies(sample, c, sl):
            cp.wait()

    @pl.when(b == 0)
    def _():
        for c in range(KV_SLOTS - 1):
            start_chunk(0, c, c)

    head_grp = lax.broadcasted_iota(jnp.int32, (N_HEADS, 1), 0) // GQA_GROUP
    qb = q_ref[0]
    sel_self = selself_ref[0, :, 0:1] > 0.0

    def step(c, carry):
        m_run, l_run, acc = carry
        sl = (first + c) % KV_SLOTS
        ahead = c + KV_SLOTS - 1
        sl_ahead = (first + ahead) % KV_SLOTS

        @pl.when(ahead < n_chunks)
        def _():
            start_chunk(b, ahead, sl_ahead)

        @pl.when(jnp.logical_and(ahead >= n_chunks, b + 1 < n_b))
        def _():
            start_chunk(b + 1, ahead - n_chunks, sl_ahead)

        wait_chunk(b, c, sl)
        s = jnp.zeros((N_HEADS, CHUNK_TOKENS), F32)
        for g in range(N_KV_HEADS):
            s_g = jnp.dot(qb, buf_ref[sl, 0, g].astype(BF16), preferred_element_type=F32)
            s = jnp.where(head_grp == g, s_g, s)
        s = s + bias_ref[0, :, _chunk_cols(c)]
        m_new = jnp.maximum(m_run, jnp.max(s, axis=1, keepdims=True))
        p = jnp.exp2(s - m_new).astype(BF16)
        scale = jnp.exp2(m_run - m_new)
        l_new = scale * l_run + jnp.sum(p.astype(F32), axis=1, keepdims=True)
        pv = jnp.zeros((N_HEADS, HEAD_DIM), F32)
        for g in range(N_KV_HEADS):
            o_g = lax.dot_general(p, buf_ref[sl, 1, g].astype(BF16), _NT, preferred_element_type=F32)
            pv = jnp.where(head_grp == g, o_g, pv)
        return m_new, l_new, scale * acc + pv

    s_self = jnp.sum(qb.astype(F32) * knew_ref[0], axis=1, keepdims=True)
    m0 = jnp.where(sel_self, s_self, NEG_BIG)
    l0 = jnp.where(sel_self, 1.0, 0.0) + jnp.zeros_like(s_self)
    _, l_fin, acc = lax.fori_loop(0, n_chunks, step, (m0, l0, l0 * vnew_ref[0]))
    o_ref[0] = acc / l_fin


def _dsa_sample(q, qi, kiw, k_new, v_new, cache_k, cache_v, cache_kidx, page_table):
    bs = q.shape[0]
    n_pages = page_table.shape[1]
    assert n_pages % PAGES_PER_CHUNK == 0 and n_pages // PAGES_PER_CHUNK >= KV_SLOTS - 1
    n_chunks = n_pages // PAGES_PER_CHUNK
    n_keys = n_pages * PAGE_SIZE
    topk = min(TOPK_MAX, (n_keys + 1) // 4)

    kidx_t = jnp.transpose(cache_kidx, (0, 2, 1))
    k_t = jnp.transpose(cache_k, (0, 2, 3, 1))
    v_t = jnp.transpose(cache_v, (0, 2, 3, 1))
    qi_h = qi.reshape(bs, IDX_HEADS, IDX_DIM)
    w_idx = (kiw[:, HEAD_DIM:HEAD_DIM + IDX_HEADS] * (IDX_DIM ** -0.5 * IDX_HEADS ** -0.5))[:, :, None]
    ki_new = jnp.broadcast_to(kiw[:, None, :IDX_DIM], (bs, IDX_HEADS, IDX_DIM))
    q_h = q.reshape(bs, N_HEADS, HEAD_DIM)
    k_rep = jnp.repeat(k_new.reshape(bs, N_KV_HEADS, HEAD_DIM), GQA_GROUP, axis=1)
    v_rep = jnp.repeat(v_new.reshape(bs, N_KV_HEADS, HEAD_DIM), GQA_GROUP, axis=1)

    per_b = lambda shape: pl.BlockSpec((1,) + shape, lambda bi, pt: (bi, 0, 0))
    any_spec = pl.BlockSpec(memory_space=pl.ANY)
    keys, key_self = pl.pallas_call(
        functools.partial(_idx_sample_kernel, n_pages=n_pages),
        grid_spec=pltpu.PrefetchScalarGridSpec(
            num_scalar_prefetch=1, grid=(bs,),
            in_specs=[per_b((IDX_HEADS, IDX_DIM)), per_b((IDX_HEADS, 1)), per_b((IDX_HEADS, IDX_DIM)),
                      any_spec],
            out_specs=[per_b((n_chunks, CHUNK_TOKENS)), per_b((1, LANES))],
            scratch_shapes=[pltpu.VMEM((2, IDX_DIM, n_keys), F32), pltpu.SemaphoreType.DMA((2,))]),
        out_shape=[jax.ShapeDtypeStruct((bs, n_chunks, CHUNK_TOKENS), jnp.int32),
                   jax.ShapeDtypeStruct((bs, 1, LANES), jnp.int32)],
        compiler_params=_params(1),
    )(page_table, qi_h, w_idx, ki_new, kidx_t)

    bias, sel_self = pl.pallas_call(
        functools.partial(_select_sample_kernel, topk=topk),
        out_shape=[jax.ShapeDtypeStruct((bs, n_keys), F32), jax.ShapeDtypeStruct((bs, LANES), F32)],
        compiler_params=pltpu.CompilerParams(vmem_limit_bytes=VMEM_LIMIT_BYTES),
    )(keys.reshape(bs, n_keys), key_self.reshape(bs, LANES))

    out = pl.pallas_call(
        functools.partial(_attn_sample_kernel, n_pages=n_pages),
        grid_spec=pltpu.PrefetchScalarGridSpec(
            num_scalar_prefetch=1, grid=(bs,),
            in_specs=[per_b((N_HEADS, HEAD_DIM)), per_b((N_HEADS, HEAD_DIM)), per_b((N_HEADS, HEAD_DIM)),
                      per_b((1, n_keys)), per_b((1, LANES)), any_spec, any_spec],
            out_specs=per_b((N_HEADS, HEAD_DIM)),
            scratch_shapes=[pltpu.VMEM((KV_SLOTS, 2, N_KV_HEADS, HEAD_DIM, CHUNK_TOKENS), F32),
                            pltpu.SemaphoreType.DMA((KV_SLOTS,))]),
        out_shape=jax.ShapeDtypeStruct((bs, N_HEADS, HEAD_DIM), F32),
        compiler_params=_params(1),
    )(page_table, q_h, k_rep, v_rep, bias.reshape(bs, 1, n_keys), sel_self.reshape(bs, 1, LANES), k_t, v_t)
    return out.reshape(bs, Q_COLS)


def _matmul_ln_kernel(a_ref, w_ref, x_ref, g_ref, b_ref, o_ref, *, alpha):
    mix = jnp.dot(a_ref[...].astype(BF16), w_ref[...], preferred_element_type=F32)
    o_ref[...] = _layernorm(alpha * x_ref[...] + mix, g_ref[...], b_ref[...])


def _matmul_ln(a, w, x, g, b, *, alpha):
    m, d = x.shape
    return pl.pallas_call(
        functools.partial(_matmul_ln_kernel, alpha=alpha),
        out_shape=jax.ShapeDtypeStruct((m, d), F32),
        compiler_params=pltpu.CompilerParams(vmem_limit_bytes=VMEM_LIMIT_BYTES),
    )(a, w, x, g.reshape(1, d), b.reshape(1, d))


HALO = 16


def _pool_mix(diffs, wp_ref, scale):
    outs = [jnp.dot(d.astype(BF16), wp_ref[g], preferred_element_type=F32) for g, d in enumerate(diffs)]
    return jnp.concatenate(outs, axis=1) * scale


def _pool_prompt_kernel(x_ref, halo_ref, wp_ref, sc_ref, g_ref, b_ref, o_ref, ext_ref,
                        *, tm, tiles_per_seq, alpha):
    i = pl.program_id(0)
    t_in_seq = i % tiles_per_seq
    x = x_ref[...]
    ext_ref[HALO:, :] = x
    ext_ref[:HALO, :] = jnp.where(t_in_seq == 0, 0.0, halo_ref[...])
    pos = t_in_seq * tm + lax.broadcasted_iota(jnp.int32, (tm, 1), 0)
    pool_ch = x.shape[1] // len(POOL_WINDOWS)
    diffs = []
    for g, w in enumerate(POOL_WINDOWS):
        ch = slice(g * pool_ch, (g + 1) * pool_ch)
        tot = x[:, ch]
        for s in range(1, w):
            tot = tot + ext_ref[HALO - s:HALO - s + tm, ch]
        cnt = jnp.minimum(pos + 1, w).astype(F32)
        diffs.append(tot / cnt - x[:, ch])
    mix = _pool_mix(diffs, wp_ref, sc_ref[...])
    o_ref[...] = _layernorm(alpha * x + mix, g_ref[...], b_ref[...])


def _pool_prompt(x, wp, scale, g, b, *, seq, tm, alpha):
    m, d = x.shape
    assert seq % tm == 0 and tm % HALO == 0
    tiles_per_seq = seq // tm
    halo_blocks = tm // HALO
    return pl.pallas_call(
        functools.partial(_pool_prompt_kernel, tm=tm, tiles_per_seq=tiles_per_seq, alpha=alpha),
        grid=(m // tm,),
        in_specs=[pl.BlockSpec((tm, d), lambda i: (i, 0)),
                  pl.BlockSpec((HALO, d), lambda i: (jnp.maximum(i * halo_blocks - 1, 0), 0)),
                  _const_spec(wp.shape), _const_spec((1, d)), _const_spec((1, d)), _const_spec((1, d))],
        out_specs=pl.BlockSpec((tm, d), lambda i: (i, 0)),
        out_shape=jax.ShapeDtypeStruct((m, d), F32),
        scratch_shapes=[pltpu.VMEM((tm + HALO, d), F32)],
        compiler_params=_params(1),
    )(x, x, wp, scale.reshape(1, d), g.reshape(1, d), b.reshape(1, d))


def _pool_sample_kernel(ext_ref, wp_ref, sc_ref, g_ref, b_ref, o_ref, *, alpha):
    n_rows = ext_ref.shape[1]
    x = ext_ref[:, n_rows - 1, :]
    pool_ch = x.shape[1] // len(POOL_WINDOWS)
    diffs = []
    for g, w in enumerate(POOL_WINDOWS):
        ch = slice(g * pool_ch, (g + 1) * pool_ch)
        tot = x[:, ch]
        for s in range(1, w):
            tot = tot + ext_ref[:, n_rows - 1 - s, ch]
        diffs.append(tot / float(w) - x[:, ch])
    mix = _pool_mix(diffs, wp_ref, sc_ref[...])
    o_ref[...] = _layernorm(alpha * x + mix, g_ref[...], b_ref[...])


def _pool_sample(ext, wp, scale, g, b, *, alpha):
    bs, n_rows, d = ext.shape
    assert n_rows >= max(POOL_WINDOWS)
    return pl.pallas_call(
        functools.partial(_pool_sample_kernel, alpha=alpha),
        out_shape=jax.ShapeDtypeStruct((bs, d), F32),
        compiler_params=pltpu.CompilerParams(vmem_limit_bytes=VMEM_LIMIT_BYTES),
    )(ext, wp, scale.reshape(1, d), g.reshape(1, d), b.reshape(1, d))


def _row_tile(m, target):
    return target if m % target == 0 else m


def kernel(x_prompt, x_sample, cache_k, cache_v, cache_kidx, state_pool, page_table, ln_g, ln_b,
           ffn1_wi, ffn1_wo, ffn2_wi, ffn2_wo, attn_w_in, attn_w_o, pool_w, pool_scale):
    batch, seq, d = x_prompt.shape
    bs, dec_seq, _ = x_sample.shape
    assert dec_seq == 1
    depth = ln_g.shape[0]
    alpha = (2 * depth) ** 0.25
    past = page_table.shape[1] * PAGE_SIZE

    xp = x_prompt.reshape(batch * seq, d)
    xs = x_sample.reshape(bs, d)
    tm_p = _row_tile(batch * seq, 512)
    tq = _row_tile(seq, 256)

    def ffn(x, wi, wo, g, b, tm):
        return _ffn_ln(x, wi.astype(BF16), wo.astype(BF16), g, b, tm=tm, alpha=alpha)

    tabs_p = _rope_tables(jnp.arange(seq, dtype=jnp.int32).astype(F32))
    tabs_s = _rope_tables((past + jnp.zeros((bs,), jnp.int32)).astype(F32))

    kp_l, vp_l, kip_l, poolp_l = [], [], [], []
    ks_l, vs_l, kis_l, pools_l = [], [], [], []
    for l in range(depth):
        j = l // N_MIXERS
        xp = ffn(xp, ffn1_wi[l], ffn1_wo[l], ln_g[l, 0], ln_b[l, 0], tm_p)
        xs = ffn(xs, ffn1_wi[l], ffn1_wo[l], ln_g[l, 0], ln_b[l, 0], bs)
        if l % N_MIXERS == 0:
            w_in = attn_w_in[j].astype(BF16)
            w_in = jnp.pad(w_in, ((0, 0), (0, W_IN_PAD - w_in.shape[1])))
            w_o = attn_w_o[j].astype(BF16)
            q, k_t, v_t, qi, kiw_t, kab, vab, kiab = _attn_proj(xp, w_in, tabs_p, tm=_row_tile(seq, 512),
                                                                attn_layouts=True)
            xp = _dsa_prompt(q, qi, kiw_t, kiab, kab, vab, xp, w_o, ln_g[l, 1], ln_b[l, 1],
                             batch=batch, seq=seq, tq=tq, alpha=alpha)
            kp_l.append(jnp.transpose(k_t, (0, 3, 1, 2)))
            vp_l.append(jnp.transpose(v_t, (0, 3, 1, 2)))
            kip_l.append(jnp.transpose(kiw_t[:, :IDX_DIM, :], (0, 2, 1)))

            qs, k_s, v_s, qis, kiws = _attn_proj(xs, w_in, tabs_s, tm=bs, attn_layouts=False)
            attn_s = _dsa_sample(qs, qis, kiws, k_s, v_s, cache_k[j], cache_v[j], cache_kidx[j],
                                 page_table)
            xs = _matmul_ln(attn_s, w_o, xs, ln_g[l, 1], ln_b[l, 1], alpha=alpha)
            ks_l.append(k_s.reshape(bs, 1, N_KV_HEADS, HEAD_DIM))
            vs_l.append(v_s.reshape(bs, 1, N_KV_HEADS, HEAD_DIM))
            kis_l.append(kiws[:, :IDX_DIM].reshape(bs, 1, IDX_DIM))
        else:
            wp = pool_w[j].astype(BF16)
            poolp_l.append(xp.reshape(batch, seq, d)[:, seq - POOL_BUF:])
            ext = jnp.concatenate([state_pool[j], xs[:, None, :]], axis=1)
            pools_l.append(ext[:, ext.shape[1] - POOL_BUF:])
            xp = _pool_prompt(xp, wp, pool_scale[j], ln_g[l, 1], ln_b[l, 1],
                              seq=seq, tm=_row_tile(seq, 512), alpha=alpha)
            xs = _pool_sample(ext, wp, pool_scale[j], ln_g[l, 1], ln_b[l, 1], alpha=alpha)
        xp = ffn(xp, ffn2_wi[l], ffn2_wo[l], ln_g[l, 2], ln_b[l, 2], tm_p)
        xs = ffn(xs, ffn2_wi[l], ffn2_wo[l], ln_g[l, 2], ln_b[l, 2], bs)

    return (xp.reshape(batch, seq, d), xs.reshape(bs, 1, d),
            jnp.stack(kp_l), jnp.stack(vp_l), jnp.stack(kip_l), jnp.stack(poolp_l),
            jnp.stack(ks_l), jnp.stack(vs_l), jnp.stack(kis_l), jnp.stack(pools_l))
```
